```python
import math
import jax
import jax.numpy as jnp
from jax import lax
import numpy as np

D_MODEL = 1024
BATCH = 2
SEQ = 16384
DEPTH = 4

GRID_W = 64
CTX_LEN = 256
SSD_HEAD_DIM = 64
SSD_INNER = D_MODEL
SSD_HEADS = SSD_INNER // SSD_HEAD_DIM
SSD_GROUPS = 4
SSD_STATE = 128
SSD_CHUNK = 128
D_CONV = 5
SSD_NORM_EPS = 1e-5
XBC_W = SSD_INNER + 2 * SSD_GROUPS * SSD_STATE
POOL_W = D_MODEL // 2
POOL_WINDOWS = (2, 4, 8, 16)
POOL_GROUPS = len(POOL_WINDOWS)
POOL_GW = POOL_W // POOL_GROUPS
GMLP_W = D_MODEL // 2
GMLP_GROUPS = 4
GMLP_GW = GMLP_W // GMLP_GROUPS
GMLP_CHUNK = 128
LN_EPS = 1e-5
N_BRANCH = 3
MOE_GROUPS = 4
EXPERTS_PER_GROUP = 4
N_EXPERTS = MOE_GROUPS * EXPERTS_PER_GROUP
MOE_TOP_K = 2
D_EXPERT = D_MODEL // 2
RMS_EPS = 1e-6
OFF_Z = 0
OFF_XBC = OFF_Z + SSD_INNER
OFF_DT = OFF_XBC + XBC_W
OFF_POOL = OFF_DT + 2 * SSD_HEADS
OFF_GMLP = OFF_POOL + POOL_W
OFF_GATE = OFF_GMLP + 2 * GMLP_W
PROJ_W = OFF_GATE + N_BRANCH * D_MODEL

kernel_name = 'hybrid_ssd_pool_gmlp_hmoe_dit'

F32 = jnp.float32


def rmsnorm(x, g, eps=RMS_EPS):
    xf = x.astype(F32)
    y = xf * lax.rsqrt(jnp.mean(xf * xf, axis=-1, keepdims=True) + eps)
    return y * g.astype(F32)


def modulated_norm(x, g, shift, scale):
    return (rmsnorm(x, g) * (1.0 + scale.astype(F32)) + shift.astype(F32)).astype(x.dtype)


def dwconv(x, w, b):
    ch = x.shape[-1]
    y = lax.conv_general_dilated(
        x, w[:, None, :].astype(x.dtype), window_strides=(1,),
        padding=[(D_CONV // 2, D_CONV // 2)],
        dimension_numbers=('NWC', 'WIO', 'NWC'), feature_group_count=ch)
    return y + b.astype(y.dtype)


def ssd_inputs(cols, conv_w, conv_b, dt_bias):
    b, l, _ = cols.shape
    xbc = jax.nn.silu(dwconv(cols[..., :XBC_W], conv_w, conv_b))
    bc_w = SSD_GROUPS * SSD_STATE
    xs = xbc[..., :SSD_INNER].reshape(b, l, SSD_HEADS, SSD_HEAD_DIM)
    bm = xbc[..., SSD_INNER:SSD_INNER + bc_w].reshape(b, l, SSD_GROUPS, SSD_STATE)
    cm = xbc[..., SSD_INNER + bc_w:].reshape(b, l, SSD_GROUPS, SSD_STATE)
    dt = jax.nn.softplus(cols[..., XBC_W:].astype(F32).reshape(b, l, 2, SSD_HEADS)
                         + dt_bias.astype(F32))
    return xs, bm, cm, dt


def ssd_scan(x, dt, a, bm, cm, h0):
    b, l, nh, p = x.shape
    g, n = bm.shape[-2:]
    k = nh // g
    nc = l // SSD_CHUNK
    L = SSD_CHUNK
    xc = x.astype(F32).reshape(b, nc, L, g, k, p)
    dtc = dt.astype(F32).reshape(b, nc, L, g, k)
    bc = bm.astype(F32).reshape(b, nc, L, g, n)
    cc = cm.astype(F32).reshape(b, nc, L, g, n)
    acum = jnp.cumsum(dtc * a.astype(F32).reshape(g, k), axis=2)
    xdt = xc * dtc[..., None]
    seg = acum[:, :, :, None] - acum[:, :, None, :]
    tri = jnp.tril(jnp.ones((L, L), dtype=bool))[:, :, None, None]
    decay = jnp.exp(jnp.where(tri, seg, -jnp.inf))
    cb = jnp.einsum('bcign,bcjgn->bcijg', cc, bc)
    y_diag = jnp.einsum('bcijgk,bcjgkp->bcigkp', cb[..., None] * decay, xdt)
    decay_end = jnp.exp(acum[:, :, -1:] - acum)
    states = jnp.einsum('bcjgn,bcjgkp->bcgkpn', bc, xdt * decay_end[..., None])
    chunk_decay = jnp.exp(acum[:, :, -1])

    def step(h, inp):
        s, d = inp
        return h * d[..., None, None] + s, h

    h_last, h_in = lax.scan(step, h0.astype(F32).reshape(b, g, k, p, n),
                            (jnp.moveaxis(states, 1, 0), jnp.moveaxis(chunk_decay, 1, 0)))
    h_in = jnp.moveaxis(h_in, 0, 1)
    y_off = jnp.einsum('bcign,bcgkpn->bcigkp', cc, h_in) * jnp.exp(acum)[..., None]
    y = (y_diag + y_off).reshape(b, l, nh, p)
    return y, h_last.reshape(b, nh, p, n)


def ssd_bidir(xs, bm, cm, dt, a, h0_f, h0_b):
    flip = lambda t: jnp.flip(t, axis=1)
    y_f, h_f = ssd_scan(xs, dt[:, :, 0], a[0], bm, cm, h0_f)
    y_b, h_b = ssd_scan(flip(xs), flip(dt[:, :, 1]), a[1], flip(bm), flip(cm), h0_b)
    return y_f + flip(y_b), h_f, h_b


def box_mean(x, w):
    n = x.shape[1]
    xf = x.astype(F32)
    cs = jnp.concatenate([jnp.zeros_like(xf[:, :1]), jnp.cumsum(xf, axis=1)], axis=1)
    pos = jnp.arange(n)
    lo = jnp.clip(pos - w // 2, 0, n)
    hi = jnp.clip(pos + (w - w // 2), 0, n)
    cnt = (hi - lo).astype(F32)
    return (cs[:, hi] - cs[:, lo]) / cnt[None, :, None]


def pool_branch(pp, pool_w, pool_scale, grid):
    b, l, _ = pp.shape
    outs = []
    for gi, w in enumerate(POOL_WINDOWS):
        seg = pp[..., gi * POOL_GW:(gi + 1) * POOL_GW]
        if grid:
            rows = l // GRID_W
            s = box_mean(seg.reshape(b * rows, GRID_W, POOL_GW), w)
            s = box_mean(s.reshape(b, rows, GRID_W * POOL_GW), w)
            s = s.reshape(b, l, POOL_GW)
        else:
            s = box_mean(seg, w)
        outs.append(s - seg.astype(F32))
    pooled = jnp.stack(outs, axis=2)
    y = jnp.einsum('blgc,gcd->blgd', pooled, pool_w.astype(F32))
    return y.reshape(b, l, POOL_W) * pool_scale.astype(F32)


def gmlp_branch(uv, ln_g, ln_b, ws, bs):
    b, l, _ = uv.shape
    act = jax.nn.gelu(uv.astype(F32))
    u, v = act[..., :GMLP_W], act[..., GMLP_W:]
    mu = jnp.mean(v, axis=-1, keepdims=True)
    var = jnp.mean(jnp.square(v - mu), axis=-1, keepdims=True)
    v = (v - mu) * lax.rsqrt(var + LN_EPS) * ln_g.astype(F32) + ln_b.astype(F32)
    vc = v.reshape(b, l // GMLP_CHUNK, GMLP_CHUNK, GMLP_GROUPS, GMLP_GW)
    mixed = jnp.einsum('gij,bnjgc->bnigc', ws.astype(F32), vc) + jnp.transpose(bs.astype(F32))[:, :, None]
    return u * mixed.reshape(b, l, GMLP_W)


def mix_stream(proj, xs, y_scan, d_skip, ssd_g, pool_w, pool_scale, ln_g, ln_b, ws, bs,
               w_br_ssd, w_br_pool, w_br_gmlp, w_o, grid):
    b, l, _ = proj.shape
    z = proj[..., OFF_Z:OFF_XBC].astype(F32)
    y = y_scan + d_skip.astype(F32)[:, None] * xs.astype(F32)
    y_ssd = rmsnorm(y.reshape(b, l, SSD_INNER) * jax.nn.silu(z), ssd_g, SSD_NORM_EPS)
    y_pool = pool_branch(proj[..., OFF_POOL:OFF_GMLP], pool_w, pool_scale, grid)
    y_gmlp = gmlp_branch(proj[..., OFF_GMLP:OFF_GATE], ln_g, ln_b, ws, bs)
    gates = jax.nn.sigmoid(proj[..., OFF_GATE:].astype(F32)).reshape(b, l, N_BRANCH, D_MODEL)
    merged = (gates[..., 0, :] * (y_ssd @ w_br_ssd.astype(F32))
              + gates[..., 1, :] * (y_pool @ w_br_pool.astype(F32))
              + gates[..., 2, :] * (y_gmlp @ w_br_gmlp.astype(F32)))
    return merged @ w_o.astype(F32)


def moe(h, w_rg, b_rg, w_re, b_re, w_e_in, w_e_out):
    b, l, _ = h.shape
    lg = jnp.einsum('bld,dg->blg', h, w_rg).astype(F32) + b_rg.astype(F32)
    pg_top, g_idx = lax.top_k(jax.nn.softmax(lg, axis=-1), 1)
    le = (jnp.einsum('bld,de->ble', h, w_re).astype(F32) + b_re.astype(F32)).reshape(
        b, l, MOE_GROUPS, EXPERTS_PER_GROUP)
    le_sel = jnp.sum(le * jax.nn.one_hot(g_idx[..., 0], MOE_GROUPS, dtype=F32)[..., None], axis=2)
    ve, e_idx = lax.top_k(le_sel, MOE_TOP_K)
    weight = pg_top * jax.nn.softmax(ve, axis=-1)
    expert_id = g_idx * EXPERTS_PER_GROUP + e_idx
    dense_w = jnp.sum(jax.nn.one_hot(expert_id, N_EXPERTS, dtype=F32) * weight[..., None], axis=2)
    out = jnp.zeros((b, l, D_MODEL), F32)
    for e in range(N_EXPERTS):
        gu = h @ w_e_in[e]
        hid = jax.nn.silu(gu[..., :D_EXPERT]) * gu[..., D_EXPERT:]
        out = out + dense_w[..., e:e + 1] * (hid @ w_e_out[e]).astype(F32)
    return out


def setup_inputs(seed: int = 0) -> dict:
    key = jax.random.key(seed)
    ks = list(jax.random.split(key, 40))

    def nrm(i, shape, scale):
        return jax.random.normal(ks[i], shape, F32) * scale

    dt0 = jnp.exp(jax.random.uniform(ks[30], (DEPTH, 2, SSD_HEADS), F32)
                  * (math.log(0.1) - math.log(0.001)) + math.log(0.001))
    return {
        'x': nrm(0, (BATCH, SEQ, D_MODEL), 1.0),
        'c': nrm(1, (BATCH, D_MODEL), 1.0),
        'ctx': nrm(2, (BATCH, CTX_LEN, D_MODEL), 1.0),
        'c_ctx': nrm(3, (D_MODEL,), 1.0),
        'w_mod': nrm(4, (DEPTH, D_MODEL, 6 * D_MODEL), 0.5 * D_MODEL ** -0.5),
        'b_mod': nrm(5, (DEPTH, 6 * D_MODEL), 0.02),
        'g_norm1': 1.0 + nrm(6, (DEPTH, D_MODEL), 0.02),
        'g_norm2': 1.0 + nrm(7, (DEPTH, D_MODEL), 0.02),
        'w_in': nrm(8, (DEPTH, D_MODEL, PROJ_W), D_MODEL ** -0.5),
        'conv_w': nrm(9, (DEPTH, D_CONV, XBC_W), D_CONV ** -0.5),
        'conv_b': nrm(10, (DEPTH, XBC_W), 0.02),
        'dt_bias': dt0 + jnp.log(-jnp.expm1(-dt0)),
        'a_log': jnp.log(jax.random.uniform(ks[31], (DEPTH, 2, SSD_HEADS), F32, 1.0, 16.0)),
        'd_skip': 1.0 + nrm(11, (DEPTH, SSD_HEADS), 0.1),
        'ssd_norm_g': 1.0 + nrm(12, (DEPTH, SSD_INNER), 0.02),
        'pool_w': nrm(13, (DEPTH, POOL_GROUPS, POOL_GW, POOL_GW), POOL_GW ** -0.5),
        'pool_scale': 1.0 + nrm(14, (DEPTH, POOL_W), 0.1),
        'gmlp_ln_g': 1.0 + nrm(15, (DEPTH, GMLP_W), 0.02),
        'gmlp_ln_b': nrm(16, (DEPTH, GMLP_W), 0.02),
        'gmlp_ws': nrm(17, (DEPTH, GMLP_GROUPS, GMLP_CHUNK, GMLP_CHUNK), GMLP_CHUNK ** -0.5),
        'gmlp_bs': 1.0 + nrm(18, (DEPTH, GMLP_GROUPS, GMLP_CHUNK), 0.02),
        'w_br_ssd': nrm(19, (DEPTH, SSD_INNER, D_MODEL), SSD_INNER ** -0.5),
        'w_br_pool': nrm(20, (DEPTH, POOL_W, D_MODEL), POOL_W ** -0.5),
        'w_br_gmlp': nrm(21, (DEPTH, GMLP_W, D_MODEL), GMLP_W ** -0.5),
        'w_o': nrm(22, (DEPTH, D_MODEL, D_MODEL), D_MODEL ** -0.5),
        'w_rg': nrm(23, (DEPTH, D_MODEL, MOE_GROUPS), D_MODEL ** -0.5),
        'b_rg': nrm(24, (DEPTH, MOE_GROUPS), 0.01),
        'w_re': nrm(25, (DEPTH, D_MODEL, N_EXPERTS), D_MODEL ** -0.5),
        'b_re': nrm(26, (DEPTH, N_EXPERTS), 0.01),
        'w_e_in': nrm(27, (DEPTH, N_EXPERTS, D_MODEL, 2 * D_EXPERT), D_MODEL ** -0.5),
        'w_e_out': nrm(28, (DEPTH, N_EXPERTS, D_EXPERT, D_MODEL), D_EXPERT ** -0.5),
        'g_final': 1.0 + nrm(29, (D_MODEL,), 0.02),
    }


def reference(x, c, ctx, c_ctx, w_mod, b_mod, g_norm1, g_norm2, w_in, conv_w, conv_b,
              dt_bias, a_log, d_skip, ssd_norm_g, pool_w, pool_scale, gmlp_ln_g, gmlp_ln_b,
              gmlp_ws, gmlp_bs, w_br_ssd, w_br_pool, w_br_gmlp, w_o, w_rg, b_rg, w_re, b_re,
              w_e_in, w_e_out, g_final):
    silu_c = jax.nn.silu(c)
    silu_cc = jax.nn.silu(c_ctx)
    for i in range(DEPTH):
        last = i == DEPTH - 1
        mod = silu_c @ w_mod[i] + b_mod[i]
        mod_c = silu_cc @ w_mod[i] + b_mod[i]
        sh1, sc1, g1, sh2, sc2, g2 = jnp.split(mod, 6, axis=-1)
        csh1, csc1, cg1, csh2, csc2, cg2 = jnp.split(mod_c, 6, axis=-1)
        a = -jnp.exp(a_log[i].astype(F32))

        h = modulated_norm(x, g_norm1[i], sh1[:, None], sc1[:, None])
        hc = modulated_norm(ctx, g_norm1[i], csh1, csc1)
        proj = h @ w_in[i]
        if last:
            ssd_cols_c = hc @ w_in[i][:, OFF_XBC:OFF_POOL]
        else:
            proj_c = hc @ w_in[i]
            ssd_cols_c = proj_c[..., OFF_XBC:OFF_POOL]

        xs_c, bm_c, cm_c, dt_c = ssd_inputs(ssd_cols_c, conv_w[i], conv_b[i], dt_bias[i])
        h0 = jnp.zeros((xs_c.shape[0], SSD_HEADS, SSD_HEAD_DIM, SSD_STATE), F32)
        y_c, hf_c, hb_c = ssd_bidir(xs_c, bm_c, cm_c, dt_c, a, h0, h0)
        xs, bm, cm, dt = ssd_inputs(proj[..., OFF_XBC:OFF_POOL], conv_w[i], conv_b[i], dt_bias[i])
        y_l, _, _ = ssd_bidir(xs, bm, cm, dt, a, hf_c, hb_c)

        mix = mix_stream(proj, xs, y_l, d_skip[i], ssd_norm_g[i], pool_w[i], pool_scale[i],
                         gmlp_ln_g[i], gmlp_ln_b[i], gmlp_ws[i], gmlp_bs[i],
                         w_br_ssd[i], w_br_pool[i], w_br_gmlp[i], w_o[i], True)
        x = x + (g1[:, None].astype(F32) * mix).astype(x.dtype)
        h2 = modulated_norm(x, g_norm2[i], sh2[:, None], sc2[:, None])
        x = x + (g2[:, None].astype(F32) * moe(h2, w_rg[i], b_rg[i], w_re[i], b_re[i],
                                                w_e_in[i], w_e_out[i])).astype(x.dtype)

        if not last:
            mix_c = mix_stream(proj_c, xs_c, y_c, d_skip[i], ssd_norm_g[i], pool_w[i],
                               pool_scale[i], gmlp_ln_g[i], gmlp_ln_b[i], gmlp_ws[i],
                               gmlp_bs[i], w_br_ssd[i], w_br_pool[i], w_br_gmlp[i], w_o[i], False)
            ctx = ctx + (cg1.astype(F32) * mix_c).astype(ctx.dtype)
            hc2 = modulated_norm(ctx, g_norm2[i], csh2, csc2)
            ctx = ctx + (cg2.astype(F32) * moe(hc2, w_rg[i], b_rg[i], w_re[i], b_re[i],
                                              w_e_in[i], w_e_out[i])).astype(ctx.dtype)
    return rmsnorm(x, g_final).astype(x.dtype)
```

```python
import functools

import numpy as np
import jax
import jax.numpy as jnp
from jax import lax
from jax.experimental import pallas as pl
from jax.experimental.pallas import tpu as pltpu

F32 = jnp.float32
BF16 = jnp.bfloat16
HIGHEST = lax.Precision.HIGHEST

LANES = 128
SUBLANES = 8
VMEM_LIMIT_BYTES = 56 * 1024 * 1024

HEAD_DIM = 64
SSD_GROUPS = 4
SSD_STATE = 128
CHUNK = 128
D_CONV = 5
POOL_WINDOWS = (2, 4, 8, 16)
GMLP_GROUPS = 4
MOE_GROUPS = 4
EXPERTS_PER_GROUP = 4
N_PAIRS = 6
N_CLASSES = MOE_GROUPS * N_PAIRS
RMS_EPS = 1e-6
LN_EPS = 1e-5
SSD_NORM_EPS = 1e-5
GRID_W = 64

LANE_W_LO = 32
LANE_W_HI = 33

EXPERT_TILE = 256
TOKEN_TILE = 256


def _params(*sem):
    return pltpu.CompilerParams(dimension_semantics=sem, vmem_limit_bytes=VMEM_LIMIT_BYTES)


def _silu(v):
    return v * (1.0 / (1.0 + jnp.exp(-v)))


def _modnorm(x, g, shift, scale):
    y = x * lax.rsqrt(jnp.mean(x * x, axis=-1, keepdims=True) + RMS_EPS)
    return y * g * (1.0 + scale) + shift


def _mod_kernel(c_ref, w_ref, b_ref, o_ref):
    s = _silu(c_ref[...])
    o_ref[0] = jnp.dot(s, w_ref[0], precision=HIGHEST, preferred_element_type=F32) + b_ref[0]


def _modulation(rows, w_mod, b_mod):
    depth, d, n = w_mod.shape
    tn = 1536
    return pl.pallas_call(
        _mod_kernel,
        grid=(depth, n // tn),
        in_specs=[
            pl.BlockSpec((SUBLANES, d), lambda i, j: (0, 0)),
            pl.BlockSpec((1, d, tn), lambda i, j: (i, 0, j)),
            pl.BlockSpec((1, 1, tn), lambda i, j: (i, 0, j)),
        ],
        out_specs=pl.BlockSpec((1, SUBLANES, tn), lambda i, j: (i, 0, j)),
        out_shape=jax.ShapeDtypeStruct((depth, SUBLANES, n), F32),
        compiler_params=_params("parallel", "parallel"),
        name="modulation",
    )(rows, w_mod, b_mod.reshape(depth, 1, n))


def _inproj_kernel(x_ref, mod_ref, g_ref, w_ref, lng_ref, lnb_ref, ws_ref, bs_ref,
                   xbc_ref, dt_ref, pp_ref, ygm_ref, *, xbc_w, pool_w, gmlp_w):
    x = x_ref[0]
    h = _modnorm(x, g_ref[...], mod_ref[0, 0:1, :], mod_ref[0, 1:2, :]).astype(BF16)
    acc = jnp.dot(h, w_ref[...], preferred_element_type=F32)
    o1 = xbc_w
    o2 = o1 + pool_w
    o3 = o2 + 2 * gmlp_w
    xbc_ref[0] = acc[:, :o1]
    pp_ref[0] = acc[:, o1:o2]
    dt_ref[0] = acc[:, o3:o3 + LANES]
    uv = acc[:, o2:o3]
    act = uv * (0.5 * (1.0 + jnp.tanh(0.7978845608028654 * (uv + 0.044715 * (uv * uv * uv)))))
    u = act[:, :gmlp_w]
    v = act[:, gmlp_w:]
    mu = jnp.mean(v, axis=-1, keepdims=True)
    vc = v - mu
    var = jnp.mean(vc * vc, axis=-1, keepdims=True)
    vn = (vc * lax.rsqrt(var + LN_EPS) * lng_ref[...] + lnb_ref[...]).astype(BF16)
    gw = gmlp_w // GMLP_GROUPS
    tm = x.shape[0]
    for c in range(tm // CHUNK):
        rows = slice(c * CHUNK, (c + 1) * CHUNK)
        for g in range(GMLP_GROUPS):
            cols = slice(g * gw, (g + 1) * gw)
            mixed = jnp.dot(ws_ref[g], vn[rows, cols], preferred_element_type=F32) + bs_ref[g]
            ygm_ref[0, rows, cols] = (u[rows, cols] * mixed).astype(BF16)


def _inproj(x, modv, g_norm, w, ln_g, ln_b, ws, bs_b, *, xbc_w, pool_w, gmlp_w):
    b, l, d = x.shape
    tm = TOKEN_TILE
    n = w.shape[1]
    kern = functools.partial(_inproj_kernel, xbc_w=xbc_w, pool_w=pool_w, gmlp_w=gmlp_w)
    tok = lambda width: pl.BlockSpec((1, tm, width), lambda i, j: (i, j, 0))
    full = lambda shape: pl.BlockSpec(shape, lambda i, j: (0,) * len(shape))
    return pl.pallas_call(
        kern,
        grid=(b, l // tm),
        in_specs=[
            tok(d),
            pl.BlockSpec((1, SUBLANES, d), lambda i, j: (i, 0, 0)),
            full((1, d)), full((d, n)), full((1, gmlp_w)), full((1, gmlp_w)),
            full(ws.shape), full(bs_b.shape),
        ],
        out_specs=[tok(xbc_w), tok(LANES), tok(pool_w), tok(gmlp_w)],
        out_shape=[
            jax.ShapeDtypeStruct((b, l, xbc_w), F32),
            jax.ShapeDtypeStruct((b, l, LANES), F32),
            jax.ShapeDtypeStruct((b, l, pool_w), F32),
            jax.ShapeDtypeStruct((b, l, gmlp_w), BF16),
        ],
        compiler_params=_params("parallel", "parallel"),
        name="inproj",
    )(x, modv, g_norm, w, ln_g, ln_b, ws, bs_b)


def _ssd_kernel(*refs, reverse, nc, inner):
    if reverse:
        (xc_ref, xp_ref, xn_ref, dt_ref, cw_ref, cb_ref, dtb_ref, a_ref, h0_ref, yf_ref, dsk_ref,
         y_ref, hout_ref, xw, xa, st) = refs
    else:
        (xc_ref, xp_ref, xn_ref, dt_ref, cw_ref, cb_ref, dtb_ref, a_ref, h0_ref,
         y_ref, hout_ref, xw, xa, st) = refs
    step = pl.program_id(1)
    ch = (nc - 1 - step) if reverse else step
    bcw = SSD_GROUPS * SSD_STATE
    heads_per_group = inner // HEAD_DIM // SSD_GROUPS
    gp = heads_per_group * HEAD_DIM
    lane0 = (inner // HEAD_DIM) if reverse else 0

    @pl.when(step == 0)
    def _():
        st[...] = h0_ref[0]

    pv = jnp.where(ch > 0, 1.0, 0.0)
    nv = jnp.where(ch < nc - 1, 1.0, 0.0)
    xw[0:SUBLANES, :] = xp_ref[0] * pv
    xw[SUBLANES:SUBLANES + CHUNK, :] = xc_ref[0]
    xw[SUBLANES + CHUNK:, :] = xn_ref[0] * nv
    cblk = 512
    for cb in range(xw.shape[1] // cblk):
        cols = slice(cb * cblk, (cb + 1) * cblk)
        acc = cb_ref[:, cols] + cw_ref[0:1, cols] * xw[SUBLANES - 2:SUBLANES - 2 + CHUNK, cols]
        for k in range(1, D_CONV):
            acc = acc + cw_ref[k:k + 1, cols] * xw[SUBLANES - 2 + k:SUBLANES - 2 + k + CHUNK, cols]
        xa[:, cols] = _silu(acc)

    z = dt_ref[0] + dtb_ref[...]
    dtm = jnp.maximum(z, 0.0) + jnp.log(1.0 + jnp.exp(-jnp.abs(z)))
    da = dtm * a_ref[...]
    ii = lax.broadcasted_iota(jnp.int32, (CHUNK, CHUNK), 0)
    jj = lax.broadcasted_iota(jnp.int32, (CHUNK, CHUNK), 1)
    causal = (ii <= jj) if reverse else (ii >= jj)
    acum = jnp.dot(causal.astype(F32), da, precision=HIGHEST, preferred_element_type=F32)
    alast = acum[0:1, :] if reverse else acum[CHUNK - 1:CHUNK, :]
    eac = jnp.exp(acum)
    wst = dtm * jnp.exp(alast - acum)
    cdec = jnp.exp(alast)
    acum_t = acum.T
    dt_t = dtm.T

    for g in range(SSD_GROUPS):
        bg = xa[:, inner + g * SSD_STATE:inner + (g + 1) * SSD_STATE]
        cg = xa[:, inner + bcw + g * SSD_STATE:inner + bcw + (g + 1) * SSD_STATE].astype(BF16)
        bgt = bg.T.astype(BF16)
        cbm = jnp.dot(cg, bgt, preferred_element_type=F32)
        sg = st[g]
        yoff = jnp.dot(cg, sg.astype(BF16), preferred_element_type=F32)
        ys, xds, decs = [], [], []
        for k in range(heads_per_group):
            hd = g * heads_per_group + k
            ln = lane0 + hd
            xs_h = xa[:, hd * HEAD_DIM:(hd + 1) * HEAD_DIM]
            seg = acum[:, ln:ln + 1] - acum_t[ln:ln + 1, :]
            dec = jnp.exp(jnp.where(causal, seg, -jnp.inf))
            m = (cbm * dec * dt_t[ln:ln + 1, :]).astype(BF16)
            yd = jnp.dot(m, xs_h.astype(BF16), preferred_element_type=F32)
            yh = yd + yoff[:, k * HEAD_DIM:(k + 1) * HEAD_DIM] * eac[:, ln:ln + 1]
            if reverse:
                cols = slice(hd * HEAD_DIM, (hd + 1) * HEAD_DIM)
                yh = yh + yf_ref[0, :, cols] + dsk_ref[:, cols] * xs_h
            ys.append(yh)
            xds.append(xs_h * wst[:, ln:ln + 1])
            decs.append(jnp.broadcast_to(cdec[:, ln:ln + 1], (1, HEAD_DIM)))
        y_ref[0, :, g * gp:(g + 1) * gp] = jnp.concatenate(ys, axis=1)
        xd = jnp.concatenate(xds, axis=1).astype(BF16)
        st[g] = sg * jnp.concatenate(decs, axis=1) + jnp.dot(bgt, xd, preferred_element_type=F32)

    @pl.when(step == nc - 1)
    def _():
        hout_ref[0] = st[...]


def _ssd_scan(xbc, dtr, conv_w8, conv_b, dtb_row, a_row, h0, *, reverse, inner, yf=None, dsk=None):
    b, l, xw_cols = xbc.shape
    nc = l // CHUNK
    hb = CHUNK // SUBLANES
    nhb = l // SUBLANES
    gp = inner // SSD_GROUPS
    chunk_of = (lambda j: nc - 1 - j) if reverse else (lambda j: j)
    cur = lambda width: pl.BlockSpec((1, CHUNK, width), lambda i, j: (i, chunk_of(j), 0))
    full = lambda shape: pl.BlockSpec(shape, lambda i, j: (0,) * len(shape))
    in_specs = [
        cur(xw_cols),
        pl.BlockSpec((1, SUBLANES, xw_cols), lambda i, j: (i, jnp.maximum(chunk_of(j) * hb - 1, 0), 0)),
        pl.BlockSpec((1, SUBLANES, xw_cols),
                     lambda i, j: (i, jnp.minimum(chunk_of(j) * hb + hb, nhb - 1), 0)),
        cur(LANES),
        full((SUBLANES, xw_cols)), full((1, xw_cols)), full((1, LANES)), full((1, LANES)),
        pl.BlockSpec((1, SSD_GROUPS, SSD_STATE, gp), lambda i, j: (i, 0, 0, 0)),
    ]
    args = [xbc, xbc, xbc, dtr, conv_w8, conv_b, dtb_row, a_row, h0]
    if reverse:
        in_specs += [cur(inner), full((1, inner))]
        args += [yf, dsk]
    kern = functools.partial(_ssd_kernel, reverse=reverse, nc=nc, inner=inner)
    return pl.pallas_call(
        kern,
        grid=(b, nc),
        in_specs=in_specs,
        out_specs=[cur(inner),
                   pl.BlockSpec((1, SSD_GROUPS, SSD_STATE, gp), lambda i, j: (i, 0, 0, 0))],
        out_shape=[jax.ShapeDtypeStruct((b, l, inner), F32),
                   jax.ShapeDtypeStruct((b, SSD_GROUPS, SSD_STATE, gp), F32)],
        scratch_shapes=[
            pltpu.VMEM((CHUNK + 2 * SUBLANES, xw_cols), F32),
            pltpu.VMEM((CHUNK, xw_cols), F32),
            pltpu.VMEM((SSD_GROUPS, SSD_STATE, gp), F32),
        ],
        compiler_params=_params("parallel", "arbitrary"),
        name="ssd_bwd" if reverse else "ssd_fwd",
    )(*args)


def _pool_kernel(pp_ref, band_ref, inv_ref, pw_ref, ps_ref, o_ref, cp, *, width, rows, slab):
    g = pl.program_id(1)
    pad = SUBLANES * width
    l = rows * width
    rows_per_slab = slab // width

    def body(window):
        lo = window // 2
        hi = window - lo
        cp[0:pad, :] = jnp.zeros((pad, LANES), F32)
        cp[pad + l:, :] = jnp.zeros((pad, LANES), F32)

        def col_pool(s, carry):
            off = pl.multiple_of(s * slab, slab)
            cp[pl.ds(pad + off, slab), :] = jnp.dot(
                band_ref[0], pp_ref[0, pl.ds(off, slab), :], precision=HIGHEST,
                preferred_element_type=F32)
            return carry

        lax.fori_loop(0, l // slab, col_pool, 0)

        def row_pool(s, carry):
            off = pl.multiple_of(s * slab, slab)
            parts = []
            for r in range(rows_per_slab):
                base = pad + off + r * width
                acc = cp[pl.ds(base - lo * width, width), :]
                for k in range(1 - lo, hi):
                    acc = acc + cp[pl.ds(base + k * width, width), :]
                parts.append(acc * inv_ref[0, pl.ds(s * rows_per_slab + r, 1), :])
            pooled = parts[0] if len(parts) == 1 else jnp.concatenate(parts, axis=0)
            d = (pooled - pp_ref[0, pl.ds(off, slab), :]).astype(BF16)
            y = jnp.dot(d, pw_ref[0], preferred_element_type=F32) * ps_ref[0]
            o_ref[0, pl.ds(off, slab), :] = y.astype(BF16)
            return carry

        lax.fori_loop(0, l // slab, row_pool, 0)

    for gi, window in enumerate(POOL_WINDOWS):
        pl.when(g == gi)(functools.partial(body, window))


def _pool_tables(l, width):
    rows = l // width
    slab = max(CHUNK, width)
    n_g = len(POOL_WINDOWS)
    band = np.zeros((n_g, slab, slab), np.float32)
    inv = np.ones((n_g, max(rows, SUBLANES), LANES), np.float32)
    for gi, w in enumerate(POOL_WINDOWS):
        lo, hi = w // 2, w - w // 2
        for t in range(slab):
            r, c = divmod(t, width)
            c0, c1 = max(c - lo, 0), min(c + hi, width)
            band[gi, t, r * width + c0:r * width + c1] = 1.0 / (c1 - c0)
        for r in range(rows):
            inv[gi, r, :] = 1.0 / (min(r + hi, rows) - max(r - lo, 0))
    return jnp.asarray(band), jnp.asarray(inv), rows, slab


def _pool_branch(pp, pool_w, pool_scale, *, width):
    b, l, pw = pp.shape
    n_g = len(POOL_WINDOWS)
    band, inv, rows, slab = _pool_tables(l, width)
    kern = functools.partial(_pool_kernel, width=width, rows=rows, slab=slab)
    return pl.pallas_call(
        kern,
        grid=(b, n_g),
        in_specs=[
            pl.BlockSpec((1, l, LANES), lambda i, g: (i, 0, g)),
            pl.BlockSpec((1, slab, slab), lambda i, g: (g, 0, 0)),
            pl.BlockSpec((1,) + inv.shape[1:], lambda i, g: (g, 0, 0)),
            pl.BlockSpec((1, LANES, LANES), lambda i, g: (g, 0, 0)),
            pl.BlockSpec((1, 1, LANES), lambda i, g: (g, 0, 0)),
        ],
        out_specs=pl.BlockSpec((1, l, LANES), lambda i, g: (i, 0, g)),
        out_shape=jax.ShapeDtypeStruct((b, l, pw), BF16),
        scratch_shapes=[pltpu.VMEM((l + 2 * SUBLANES * width, LANES), F32)],
        compiler_params=_params("parallel", "parallel"),
        name="pool",
    )(pp, band, inv, pool_w, pool_scale.reshape(n_g, 1, LANES))


def _mix_kernel(x_ref, mod_ref, g1_ref, g2_ref, wz_ref, wg_ref, y_ref, sg_ref, yp_ref, ygm_ref,
                wbs_ref, wbp_ref, wbg_ref, wo_ref, wr_ref, br_ref, x1_ref, h2_ref):
    x = x_ref[0]
    d = x.shape[1]
    h = _modnorm(x, g1_ref[...], mod_ref[0, 0:1, :], mod_ref[0, 1:2, :]).astype(BF16)
    z = jnp.dot(h, wz_ref[...], preferred_element_type=F32)
    t = y_ref[0] * _silu(z)
    ys = t * lax.rsqrt(jnp.mean(t * t, axis=-1, keepdims=True) + SSD_NORM_EPS) * sg_ref[...]
    gates = 1.0 / (1.0 + jnp.exp(-jnp.dot(h, wg_ref[...], preferred_element_type=F32)))
    merged = gates[:, :d] * jnp.dot(ys.astype(BF16), wbs_ref[...], preferred_element_type=F32)
    merged = merged + gates[:, d:2 * d] * jnp.dot(yp_ref[0], wbp_ref[...], preferred_element_type=F32)
    merged = merged + gates[:, 2 * d:] * jnp.dot(ygm_ref[0], wbg_ref[...], preferred_element_type=F32)
    mix = jnp.dot(merged.astype(BF16), wo_ref[...], preferred_element_type=F32)
    x1 = x + mod_ref[0, 2:3, :] * mix
    x1_ref[0] = x1
    h2 = _modnorm(x1, g2_ref[...], mod_ref[0, 3:4, :], mod_ref[0, 4:5, :])
    h2_ref[0, :, :d] = h2

    lg = jnp.dot(h2, wr_ref[...], precision=HIGHEST, preferred_element_type=F32) + br_ref[...]
    lane = lax.broadcasted_iota(jnp.int32, lg.shape, 1).astype(F32)
    ninf = -jnp.inf
    big = float(LANES)
    gl = jnp.where(lane < MOE_GROUPS, lg, ninf)
    gmax = jnp.max(gl, axis=-1, keepdims=True)
    pg_top = 1.0 / jnp.sum(jnp.exp(gl - gmax), axis=-1, keepdims=True)
    gidx = jnp.min(jnp.where(gl == gmax, lane, big), axis=-1, keepdims=True)
    e0 = MOE_GROUPS + EXPERTS_PER_GROUP * gidx
    el = jnp.where((lane >= e0) & (lane < e0 + EXPERTS_PER_GROUP), lg, ninf)
    v1 = jnp.max(el, axis=-1, keepdims=True)
    i1 = jnp.min(jnp.where(el == v1, lane, big), axis=-1, keepdims=True)
    el2 = jnp.where(lane == i1, ninf, el)
    v2 = jnp.max(el2, axis=-1, keepdims=True)
    i2 = jnp.min(jnp.where(el2 == v2, lane, big), axis=-1, keepdims=True)
    r = jnp.exp(v2 - v1)
    wt1 = pg_top * (1.0 / (1.0 + r))
    wt2 = pg_top * (r / (1.0 + r))
    first_lo = i1 < i2
    lo = jnp.where(first_lo, i1, i2) - e0
    hi = jnp.where(first_lo, i2, i1) - e0
    w_lo = jnp.where(first_lo, wt1, wt2)
    w_hi = jnp.where(first_lo, wt2, wt1)
    cls = gidx * N_PAIRS + lo * (7.0 - lo) * 0.5 + (hi - lo - 1.0)
    info = jnp.where(lane == cls, 1.0, 0.0)
    info = jnp.where(lane == LANE_W_LO, w_lo, info)
    info = jnp.where(lane == LANE_W_HI, w_hi, info)
    h2_ref[0, :, d:] = info


def _mix(x, modv, g1, g2, wz, wg, y, ssd_g, yp, ygm, wbs, wbp, wbg, wo, wr, br):
    b, l, d = x.shape
    tm = TOKEN_TILE
    tok = lambda width: pl.BlockSpec((1, tm, width), lambda i, j: (i, j, 0))
    full = lambda a: pl.BlockSpec(a.shape, lambda i, j: (0,) * a.ndim)
    return pl.pallas_call(
        _mix_kernel,
        grid=(b, l // tm),
        in_specs=[
            tok(d), pl.BlockSpec((1, SUBLANES, d), lambda i, j: (i, 0, 0)),
            full(g1), full(g2), full(wz), full(wg), tok(y.shape[2]), full(ssd_g),
            tok(yp.shape[2]), tok(ygm.shape[2]), full(wbs), full(wbp), full(wbg), full(wo),
            full(wr), full(br),
        ],
        out_specs=[tok(d), tok(d + LANES)],
        out_shape=[jax.ShapeDtypeStruct((b, l, d), F32),
                   jax.ShapeDtypeStruct((b, l, d + LANES), F32)],
        compiler_params=_params("parallel", "parallel"),
        name="mix",
    )(x, modv, g1, g2, wz, wg, y, ssd_g, yp, ygm, wbs, wbp, wbg, wo, wr, br)


def _sort_kernel(info_ref, pos_ref, tcls_ref, cnt, base, *, tile, n_slabs, n_tile_lanes):
    phase = pl.program_id(0)
    i = pl.program_id(1)
    ts = info_ref.shape[0]
    sub = lax.broadcasted_iota(jnp.int32, (LANES, LANES), 0)
    lan = lax.broadcasted_iota(jnp.int32, (LANES, LANES), 1)

    oh_t = jnp.concatenate(
        [info_ref[k * LANES:(k + 1) * LANES, :].T for k in range(n_slabs)], axis=1)
    csub = lax.broadcasted_iota(jnp.int32, oh_t.shape, 0)
    oh_t = jnp.where(csub < N_CLASSES, oh_t, 0.0)
    tile_count = jnp.sum(oh_t, axis=-1, keepdims=True)

    @pl.when((phase == 0) & (i == 0))
    def _():
        cnt[...] = jnp.zeros_like(cnt)

    @pl.when(phase == 0)
    def _():
        cnt[...] += jnp.broadcast_to(tile_count, cnt.shape)

    @pl.when((phase == 1) & (i == 0))
    def _():
        padded = jnp.ceil(cnt[...] * (1.0 / tile)) * tile
        starts = jnp.dot((lan < sub).astype(F32), padded, precision=HIGHEST,
                         preferred_element_type=F32)
        base[...] = starts
        tcls_ref[...] = jnp.zeros_like(tcls_ref)

    @pl.when(phase == 1)
    def _():
        ti = lax.broadcasted_iota(jnp.int32, (ts, ts), 0)
        tj = lax.broadcasted_iota(jnp.int32, (ts, ts), 1)
        before = jnp.dot(oh_t.astype(BF16), (ti < tj).astype(BF16), preferred_element_type=F32)
        posv = jnp.sum(oh_t * (before + base[:, 0:1]), axis=0, keepdims=True)
        pos_ref[0] = posv.astype(jnp.int32)
        padded = jnp.ceil(cnt[...] * (1.0 / tile)) * tile
        starts = jnp.dot((lan < sub).astype(F32), padded, precision=HIGHEST,
                         preferred_element_type=F32)
        ends = (starts + padded)[:, 0:1]
        tl = lax.broadcasted_iota(jnp.int32, (LANES, n_tile_lanes), 1).astype(F32) * tile
        cs = lax.broadcasted_iota(jnp.int32, (LANES, n_tile_lanes), 0)
        done = jnp.where((cs < N_CLASSES) & (ends <= tl), 1.0, 0.0)
        tc = jnp.minimum(jnp.sum(done, axis=0, keepdims=True), N_CLASSES - 1.0)
        total = jnp.max(jnp.where(cs < N_CLASSES, ends, 0.0), axis=0, keepdims=True)
        valid = jnp.where(tl[0:1, :] < total, 1.0, 0.0)
        tcls_ref[0, 0:1, :] = tc.astype(jnp.int32)
        tcls_ref[0, 1:2, :] = valid.astype(jnp.int32)
        base[...] += jnp.broadcast_to(tile_count, base.shape)


def _route_sort(h2ext, *, d, n_exp_tiles):
    t = h2ext.shape[0]
    ts = min(1024, t)
    nt = t // ts
    n_tile_lanes = -(-n_exp_tiles // LANES) * LANES
    kern = functools.partial(_sort_kernel, tile=EXPERT_TILE, n_slabs=ts // LANES,
                             n_tile_lanes=n_tile_lanes)
    pos, tcls = pl.pallas_call(
        kern,
        grid=(2, nt),
        in_specs=[pl.BlockSpec((ts, LANES), lambda p, i: (i, d // LANES))],
        out_specs=[pl.BlockSpec((1, 1, ts), lambda p, i: (i * p, 0, 0)),
                   pl.BlockSpec((1, SUBLANES, n_tile_lanes), lambda p, i: (0, 0, 0))],
        out_shape=[jax.ShapeDtypeStruct((nt, 1, ts), jnp.int32),
                   jax.ShapeDtypeStruct((1, SUBLANES, n_tile_lanes), jnp.int32)],
        scratch_shapes=[pltpu.VMEM((LANES, LANES), F32), pltpu.VMEM((LANES, LANES), F32)],
        compiler_params=_params("arbitrary", "arbitrary"),
        name="route_sort",
    )(h2ext)
    return pos.reshape(t), tcls[0, 0, :n_exp_tiles], tcls[0, 1, :n_exp_tiles]


def _dispatch_kernel(pos_ref, rows_ref, init_ref, out_ref, sem):
    del init_ref
    n = rows_ref.shape[0]

    def issue(r, carry):
        pltpu.make_async_copy(rows_ref.at[pl.ds(r, 1)], out_ref.at[pl.ds(pos_ref[0, 0, r], 1)],
                              sem).start()
        return carry

    lax.fori_loop(0, n, issue, 0, unroll=8)
    pltpu.make_async_copy(rows_ref, out_ref.at[pl.ds(0, n)], sem).wait()


def _dispatch(h2ext, pos, *, n_sorted_rows):
    t, w = h2ext.shape
    tm = TOKEN_TILE
    nt = t // tm
    return pl.pallas_call(
        _dispatch_kernel,
        grid=(nt,),
        in_specs=[
            pl.BlockSpec((1, 1, tm), lambda i: (i, 0, 0), memory_space=pltpu.SMEM),
            pl.BlockSpec((tm, w), lambda i: (i, 0)),
            pl.BlockSpec(memory_space=pl.ANY),
        ],
        out_specs=pl.BlockSpec(memory_space=pl.ANY),
        out_shape=jax.ShapeDtypeStruct((n_sorted_rows, w), F32),
        scratch_shapes=[pltpu.SemaphoreType.DMA],
        input_output_aliases={2: 0},
        compiler_params=_params("arbitrary"),
        name="dispatch",
    )(pos.reshape(nt, 1, tm), h2ext, jnp.zeros((n_sorted_rows, w), F32))


def _expert_kernel(elo_ref, ehi_ref, valid_ref, rows_ref, wil_ref, wih_ref, wol_ref, woh_ref,
                   o_ref, *, d, d_expert):
    del elo_ref, ehi_ref
    i = pl.program_id(0)

    @pl.when(valid_ref[i] == 0)
    def _():
        o_ref[...] = jnp.zeros_like(o_ref)

    @pl.when(valid_ref[i] != 0)
    def _():
        h = rows_ref[:, :d].astype(BF16)
        info = rows_ref[:, d:]

        def expert(wi_ref, wo_ref):
            gu = jnp.dot(h, wi_ref[0], preferred_element_type=F32)
            hid = _silu(gu[:, :d_expert]) * gu[:, d_expert:]
            return jnp.dot(hid.astype(BF16), wo_ref[0], preferred_element_type=F32)

        o_ref[...] = (info[:, LANE_W_LO:LANE_W_LO + 1] * expert(wil_ref, wol_ref)
                      + info[:, LANE_W_HI:LANE_W_HI + 1] * expert(wih_ref, woh_ref))


def _experts(sorted_rows, e_lo, e_hi, valid, w_e_in, w_e_out, *, d):
    n_rows, w = sorted_rows.shape
    tm = EXPERT_TILE
    n_tiles = n_rows // tm
    d_expert = w_e_out.shape[1]
    kern = functools.partial(_expert_kernel, d=d, d_expert=d_expert)
    grid_spec = pltpu.PrefetchScalarGridSpec(
        num_scalar_prefetch=3,
        grid=(n_tiles,),
        in_specs=[
            pl.BlockSpec((tm, w), lambda i, lo, hi, v: (i, 0)),
            pl.BlockSpec((1,) + w_e_in.shape[1:], lambda i, lo, hi, v: (lo[i], 0, 0)),
            pl.BlockSpec((1,) + w_e_in.shape[1:], lambda i, lo, hi, v: (hi[i], 0, 0)),
            pl.BlockSpec((1,) + w_e_out.shape[1:], lambda i, lo, hi, v: (lo[i], 0, 0)),
            pl.BlockSpec((1,) + w_e_out.shape[1:], lambda i, lo, hi, v: (hi[i], 0, 0)),
        ],
        out_specs=pl.BlockSpec((tm, d), lambda i, lo, hi, v: (i, 0)),
    )
    return pl.pallas_call(
        kern,
        grid_spec=grid_spec,
        out_shape=jax.ShapeDtypeStruct((n_rows, d), F32),
        compiler_params=_params("arbitrary"),
        name="experts",
    )(e_lo, e_hi, valid, sorted_rows, w_e_in, w_e_in, w_e_out, w_e_out)


def _combine_kernel(pos_ref, x1_ref, mod_ref, gf_ref, moe_ref, o_ref, rows, sem, *, final):
    n = rows.shape[0]

    def issue(r, carry):
        pltpu.make_async_copy(moe_ref.at[pl.ds(pos_ref[0, 0, r], 1)], rows.at[pl.ds(r, 1)],
                              sem).start()
        return carry

    lax.fori_loop(0, n, issue, 0, unroll=8)
    pltpu.make_async_copy(moe_ref.at[pl.ds(0, n)], rows, sem).wait()
    x2 = x1_ref[...] + mod_ref[0, 5:6, :] * rows[...]
    if final:
        x2 = x2 * lax.rsqrt(jnp.mean(x2 * x2, axis=-1, keepdims=True) + RMS_EPS) * gf_ref[...]
    o_ref[...] = x2


def _combine(x1, pos, modv, g_final, moe_sorted, *, seq_len, final):
    t, d = x1.shape
    tm = TOKEN_TILE
    nt = t // tm
    tiles_per_seq = seq_len // tm
    kern = functools.partial(_combine_kernel, final=final)
    return pl.pallas_call(
        kern,
        grid=(nt,),
        in_specs=[
            pl.BlockSpec((1, 1, tm), lambda i: (i, 0, 0), memory_space=pltpu.SMEM),
            pl.BlockSpec((tm, d), lambda i: (i, 0)),
            pl.BlockSpec((1, SUBLANES, d), lambda i: (i // tiles_per_seq, 0, 0)),
            pl.BlockSpec((1, d), lambda i: (0, 0)),
            pl.BlockSpec(memory_space=pl.ANY),
        ],
        out_specs=pl.BlockSpec((tm, d), lambda i: (i, 0)),
        out_shape=jax.ShapeDtypeStruct((t, d), F32),
        scratch_shapes=[pltpu.VMEM((tm, d), F32), pltpu.SemaphoreType.DMA],
        compiler_params=_params("arbitrary"),
        name="combine",
    )(pos.reshape(nt, 1, tm), x1, modv, g_final, moe_sorted)


def _pair_tables():
    lo, hi = [], []
    for g in range(MOE_GROUPS):
        for a in range(EXPERTS_PER_GROUP):
            for b in range(a + 1, EXPERTS_PER_GROUP):
                lo.append(g * EXPERTS_PER_GROUP + a)
                hi.append(g * EXPERTS_PER_GROUP + b)
    return jnp.asarray(lo, jnp.int32), jnp.asarray(hi, jnp.int32)


def _stream_layer(x, modv, lw, h0f, h0b, *, width, final, g_final, need_tokens=True):
    b, l, d = x.shape
    inner = lw["inner"]
    xbc, dtr, pp, ygm = _inproj(x, modv, lw["g1"], lw["w_ssd"], lw["ln_g"], lw["ln_b"], lw["ws"],
                                lw["bs_b"], xbc_w=lw["xbc_w"], pool_w=lw["pool_cols"],
                                gmlp_w=lw["gmlp_w"])
    yf, hf = _ssd_scan(xbc, dtr, lw["conv_w8"], lw["conv_b"], lw["dtb_row"], lw["a_row"], h0f,
                       reverse=False, inner=inner)
    y, hb = _ssd_scan(xbc, dtr, lw["conv_w8"], lw["conv_b"], lw["dtb_row"], lw["a_row"], h0b,
                      reverse=True, inner=inner, yf=yf, dsk=lw["dsk"])
    if not need_tokens:
        return None, hf, hb
    yp = _pool_branch(pp, lw["pool_w"], lw["pool_scale"], width=width)
    x1, h2ext = _mix(x, modv, lw["g1"], lw["g2"], lw["wz"], lw["wg"], y, lw["ssd_g"], yp, ygm,
                     lw["wbs"], lw["wbp"], lw["wbg"], lw["wo"], lw["wr"], lw["br"])
    t = b * l
    h2ext = h2ext.reshape(t, d + LANES)
    n_exp_tiles = t // EXPERT_TILE + N_CLASSES
    pos, tcls, valid = _route_sort(h2ext, d=d, n_exp_tiles=n_exp_tiles)
    pair_lo, pair_hi = _pair_tables()
    sorted_rows = _dispatch(h2ext, pos, n_sorted_rows=n_exp_tiles * EXPERT_TILE)
    moe_sorted = _experts(sorted_rows, pair_lo[tcls], pair_hi[tcls], valid, lw["w_e_in"],
                          lw["w_e_out"], d=d)
    x2 = _combine(x1.reshape(t, d), pos, modv, g_final, moe_sorted, seq_len=l, final=final)
    return x2.reshape(b, l, d), hf, hb


def kernel(x, c, ctx, c_ctx, w_mod, b_mod, g_norm1, g_norm2, w_in, conv_w, conv_b, dt_bias, a_log,
           d_skip, ssd_norm_g, pool_w, pool_scale, gmlp_ln_g, gmlp_ln_b, gmlp_ws, gmlp_bs, w_br_ssd,
           w_br_pool, w_br_gmlp, w_o, w_rg, b_rg, w_re, b_re, w_e_in, w_e_out, g_final):
    b, seq, d = x.shape
    depth = w_mod.shape[0]
    n_heads = d_skip.shape[1]
    inner = n_heads * HEAD_DIM
    bcw = SSD_GROUPS * SSD_STATE
    xbc_w = inner + 2 * bcw
    pool_cols = pool_w.shape[1] * pool_w.shape[2]
    gmlp_w = gmlp_ln_g.shape[1]
    off_xbc = inner
    off_dt = off_xbc + xbc_w
    off_pool = off_dt + 2 * n_heads
    off_gmlp = off_pool + pool_cols
    off_gate = off_gmlp + 2 * gmlp_w

    rows = jnp.zeros((SUBLANES, d), F32).at[:b].set(c).at[b].set(c_ctx)
    mod = _modulation(rows, w_mod, b_mod)
    gp = inner // SSD_GROUPS
    zero_state = jnp.zeros((b, SSD_GROUPS, SSD_STATE, gp), F32)
    g_final_row = g_final.reshape(1, d)

    def pad_rows(m):
        return jnp.pad(m.reshape(m.shape[0], 6, d), ((0, 0), (0, 2), (0, 0)))

    for i in range(depth):
        last = i == depth - 1
        mod_lat = pad_rows(mod[i, :b])
        mod_ctx = pad_rows(jnp.broadcast_to(mod[i, b:b + 1], (b, 6 * d)))
        wi = w_in[i]
        dt_pad = jnp.zeros((d, LANES - 2 * n_heads), F32)
        a = -jnp.exp(a_log[i].astype(F32)).reshape(1, 2 * n_heads)
        lane_pad = ((0, 0), (0, LANES - 2 * n_heads))
        wr = jnp.pad(jnp.concatenate([w_rg[i], w_re[i]], axis=1),
                     ((0, 0), (0, LANES - MOE_GROUPS - MOE_GROUPS * EXPERTS_PER_GROUP)))
        br = jnp.pad(jnp.concatenate([b_rg[i], b_re[i]]).reshape(1, -1),
                     ((0, 0), (0, LANES - MOE_GROUPS - MOE_GROUPS * EXPERTS_PER_GROUP)))
        lw = dict(
            inner=inner, xbc_w=xbc_w, pool_cols=pool_cols, gmlp_w=gmlp_w,
            g1=g_norm1[i].reshape(1, d), g2=g_norm2[i].reshape(1, d),
            w_ssd=jnp.concatenate([wi[:, off_xbc:off_dt], wi[:, off_pool:off_gate],
                                   wi[:, off_dt:off_pool], dt_pad], axis=1).astype(BF16),
            wz=wi[:, :inner].astype(BF16), wg=wi[:, off_gate:].astype(BF16),
            ln_g=gmlp_ln_g[i].reshape(1, gmlp_w), ln_b=gmlp_ln_b[i].reshape(1, gmlp_w),
            ws=gmlp_ws[i].astype(BF16),
            bs_b=jnp.broadcast_to(gmlp_bs[i][:, :, None], gmlp_bs[i].shape + (LANES,)),
            conv_w8=jnp.pad(conv_w[i], ((0, SUBLANES - D_CONV), (0, 0))),
            conv_b=conv_b[i].reshape(1, xbc_w),
            dtb_row=jnp.pad(dt_bias[i].reshape(1, 2 * n_heads), lane_pad),
            a_row=jnp.pad(a, lane_pad),
            dsk=jnp.repeat(d_skip[i], HEAD_DIM).reshape(1, inner),
            ssd_g=ssd_norm_g[i].reshape(1, inner),
            pool_w=pool_w[i].astype(BF16), pool_scale=pool_scale[i],
            wbs=w_br_ssd[i].astype(BF16), wbp=w_br_pool[i].astype(BF16),
            wbg=w_br_gmlp[i].astype(BF16), wo=w_o[i].astype(BF16),
            wr=wr, br=br,
            w_e_in=w_e_in[i].astype(BF16), w_e_out=w_e_out[i].astype(BF16),
        )
        ctx_new, hf_c, hb_c = _stream_layer(ctx, mod_ctx, lw, zero_state, zero_state,
                                            width=ctx.shape[1], final=False, g_final=g_final_row,
                                            need_tokens=not last)
        x, _, _ = _stream_layer(x, mod_lat, lw, hf_c, hb_c, width=GRID_W, final=last,
                                g_final=g_final_row)
        if not last:
            ctx = ctx_new
    return x
```

```python
import functools

import numpy as np
import jax
import jax.numpy as jnp
from jax import lax
from jax.experimental import pallas as pl
from jax.experimental.pallas import tpu as pltpu

F32 = jnp.float32
BF16 = jnp.bfloat16
HIGHEST = lax.Precision.HIGHEST

LANES = 128
SUBLANES = 8
VMEM_LIMIT_BYTES = 56 * 1024 * 1024

HEAD_DIM = 64
SSD_GROUPS = 4
SSD_STATE = 128
CHUNK = 128
D_CONV = 5
POOL_WINDOWS = (2, 4, 8, 16)
GMLP_GROUPS = 4
MOE_GROUPS = 4
EXPERTS_PER_GROUP = 4
N_PAIRS = 6
N_CLASSES = MOE_GROUPS * N_PAIRS
RMS_EPS = 1e-6
LN_EPS = 1e-5
SSD_NORM_EPS = 1e-5
GRID_W = 64

LANE_W_LO = 32
LANE_W_HI = 33

EXPERT_TILE = 256
TOKEN_TILE = 256
MATMUL_TILE = 512


def _params(*sem):
    return pltpu.CompilerParams(dimension_semantics=sem, vmem_limit_bytes=VMEM_LIMIT_BYTES)


def _sigmoid(v):
    return 0.5 * (1.0 + jnp.tanh(0.5 * v))


def _silu(v):
    return v * _sigmoid(v)


def _resident(shape):
    return pl.BlockSpec(shape, lambda *_: (0,) * len(shape), pipeline_mode=pl.Buffered(1))


def _modnorm(x, g, shift, scale):
    y = x * lax.rsqrt(jnp.mean(x * x, axis=-1, keepdims=True) + RMS_EPS)
    return y * g * (1.0 + scale) + shift


def _mod_kernel(c_ref, w_ref, b_ref, o_ref):
    s = _silu(c_ref[...])
    o_ref[0] = jnp.dot(s, w_ref[0], precision=HIGHEST, preferred_element_type=F32) + b_ref[0]


def _modulation(rows, w_mod, b_mod):
    depth, d, n = w_mod.shape
    tn = 1536
    return pl.pallas_call(
        _mod_kernel,
        grid=(depth, n // tn),
        in_specs=[
            pl.BlockSpec((SUBLANES, d), lambda i, j: (0, 0)),
            pl.BlockSpec((1, d, tn), lambda i, j: (i, 0, j)),
            pl.BlockSpec((1, 1, tn), lambda i, j: (i, 0, j)),
        ],
        out_specs=pl.BlockSpec((1, SUBLANES, tn), lambda i, j: (i, 0, j)),
        out_shape=jax.ShapeDtypeStruct((depth, SUBLANES, n), F32),
        compiler_params=_params("parallel", "parallel"),
        name="modulation",
    )(rows, w_mod, b_mod.reshape(depth, 1, n))


def _inproj_kernel(x_ref, mod_ref, g_ref, w_ref, lng_ref, lnb_ref, ws_ref, bs_ref,
                   xbc_ref, dt_ref, pp_ref, ygm_ref, *, xbc_w, pool_w, gmlp_w):
    x = x_ref[0]
    h = _modnorm(x, g_ref[...], mod_ref[0, 0:1, :], mod_ref[0, 1:2, :]).astype(BF16)
    acc = jnp.dot(h, w_ref[...], preferred_element_type=F32)
    o1 = xbc_w
    o2 = o1 + pool_w
    o3 = o2 + 2 * gmlp_w
    xbc_ref[0] = acc[:, :o1]
    pp_ref[0] = acc[:, o1:o2]
    dt_ref[0] = acc[:, o3:o3 + LANES]
    uv = acc[:, o2:o3]
    act = uv * (0.5 * (1.0 + jnp.tanh(0.7978845608028654 * (uv + 0.044715 * (uv * uv * uv)))))
    u = act[:, :gmlp_w]
    v = act[:, gmlp_w:]
    mu = jnp.mean(v, axis=-1, keepdims=True)
    vc = v - mu
    var = jnp.mean(vc * vc, axis=-1, keepdims=True)
    vn = (vc * lax.rsqrt(var + LN_EPS) * lng_ref[...] + lnb_ref[...]).astype(BF16)
    gw = gmlp_w // GMLP_GROUPS
    tm = x.shape[0]
    for c in range(tm // CHUNK):
        rows = slice(c * CHUNK, (c + 1) * CHUNK)
        for g in range(GMLP_GROUPS):
            cols = slice(g * gw, (g + 1) * gw)
            mixed = jnp.dot(ws_ref[g], vn[rows, cols], preferred_element_type=F32) + bs_ref[g]
            ygm_ref[0, rows, cols] = (u[rows, cols] * mixed).astype(BF16)


def _inproj(x, modv, g_norm, w, ln_g, ln_b, ws, bs_b, *, xbc_w, pool_w, gmlp_w):
    b, l, d = x.shape
    tm = min(MATMUL_TILE, l)
    n = w.shape[1]
    kern = functools.partial(_inproj_kernel, xbc_w=xbc_w, pool_w=pool_w, gmlp_w=gmlp_w)
    tok = lambda width: pl.BlockSpec((1, tm, width), lambda i, j: (i, j, 0))
    full = _resident
    return pl.pallas_call(
        kern,
        grid=(b, l // tm),
        in_specs=[
            tok(d),
            pl.BlockSpec((1, SUBLANES, d), lambda i, j: (i, 0, 0)),
            full((1, d)), full((d, n)), full((1, gmlp_w)), full((1, gmlp_w)),
            full(ws.shape), full(bs_b.shape),
        ],
        out_specs=[tok(xbc_w), tok(LANES), tok(pool_w), tok(gmlp_w)],
        out_shape=[
            jax.ShapeDtypeStruct((b, l, xbc_w), F32),
            jax.ShapeDtypeStruct((b, l, LANES), F32),
            jax.ShapeDtypeStruct((b, l, pool_w), F32),
            jax.ShapeDtypeStruct((b, l, gmlp_w), BF16),
        ],
        compiler_params=_params("parallel", "parallel"),
        name="inproj",
    )(x, modv, g_norm, w, ln_g, ln_b, ws, bs_b)


def _ssd_kernel(*refs, reverse, nc, inner):
    if reverse:
        (xab_ref, dt_ref, dtb_ref, a_ref, exp_ref, h0_ref, yf_ref, y_ref, hout_ref, st) = refs
        xa = None
    else:
        (xc_ref, xp_ref, xn_ref, dt_ref, cw_ref, cb_ref, dtb_ref, a_ref, exp_ref, h0_ref, dsk_ref,
         y_ref, xab_ref, hout_ref, xw, xa, st) = refs
    step = pl.program_id(1)
    bcw = SSD_GROUPS * SSD_STATE
    heads_per_group = inner // HEAD_DIM // SSD_GROUPS
    gp = heads_per_group * HEAD_DIM
    lane0 = (inner // HEAD_DIM) if reverse else 0

    @pl.when(step == 0)
    def _():
        st[...] = h0_ref[0]

    if not reverse:
        pv = jnp.where(step > 0, 1.0, 0.0)
        nv = jnp.where(step < nc - 1, 1.0, 0.0)
        xw[0:SUBLANES, :] = xp_ref[0] * pv
        xw[SUBLANES:SUBLANES + CHUNK, :] = xc_ref[0]
        xw[SUBLANES + CHUNK:, :] = xn_ref[0] * nv
        cblk = 512
        for cb in range(xw.shape[1] // cblk):
            cols = slice(cb * cblk, (cb + 1) * cblk)
            acc = cb_ref[:, cols] + cw_ref[0:1, cols] * xw[SUBLANES - 2:SUBLANES - 2 + CHUNK, cols]
            for k in range(1, D_CONV):
                acc = acc + cw_ref[k:k + 1, cols] * xw[SUBLANES - 2 + k:SUBLANES - 2 + k + CHUNK, cols]
            act = _silu(acc)
            xa[:, cols] = act
            xab_ref[0, :, cols] = act.astype(BF16)

    def cols_f32(lo, hi):
        return xab_ref[0, :, lo:hi].astype(F32) if reverse else xa[:, lo:hi]

    def cols_bf16(lo, hi):
        return xab_ref[0, :, lo:hi] if reverse else xa[:, lo:hi].astype(BF16)

    z = dt_ref[0] + dtb_ref[...]
    dtm = jnp.maximum(z, 0.0) + jnp.log(1.0 + jnp.exp(-jnp.abs(z)))
    da = dtm * a_ref[...]
    ii = lax.broadcasted_iota(jnp.int32, (CHUNK, CHUNK), 0)
    jj = lax.broadcasted_iota(jnp.int32, (CHUNK, CHUNK), 1)
    causal = (ii <= jj) if reverse else (ii >= jj)
    acum = jnp.dot(causal.astype(F32), da, precision=HIGHEST, preferred_element_type=F32)
    alast = acum[0:1, :] if reverse else acum[CHUNK - 1:CHUNK, :]
    acum_t = acum.T
    dt_t = dtm.T

    def per_channel(v):
        v_hi = v.astype(BF16)
        v_lo = (v - v_hi.astype(F32)).astype(BF16)
        return (jnp.dot(v_hi, exp_ref[...], preferred_element_type=F32)
                + jnp.dot(v_lo, exp_ref[...], preferred_element_type=F32))

    eac_c = per_channel(jnp.exp(acum))
    wst_c = per_channel(dtm * jnp.exp(alast - acum))
    last_row = 0 if reverse else CHUNK - 1
    head_of_lane = lax.broadcasted_iota(jnp.int32, (CHUNK, gp), 1) // HEAD_DIM

    for g in range(SSD_GROUPS):
        b0 = inner + g * SSD_STATE
        c0 = inner + bcw + g * SSD_STATE
        gcols = slice(g * gp, (g + 1) * gp)
        cg = cols_bf16(c0, c0 + SSD_STATE)
        bgt = cols_f32(b0, b0 + SSD_STATE).T.astype(BF16)
        cbm = jnp.dot(cg, bgt, preferred_element_type=F32)
        sg = st[g]
        yoff = jnp.dot(cg, sg.astype(BF16), preferred_element_type=F32)
        xs_f = cols_f32(g * gp, (g + 1) * gp)
        xs_b = cols_bf16(g * gp, (g + 1) * gp)
        ms, xbd = [], []
        for k in range(heads_per_group):
            ln = lane0 + g * heads_per_group + k
            seg = acum[:, ln:ln + 1] - acum_t[ln:ln + 1, :]
            dec = jnp.exp(jnp.where(causal, seg, -jnp.inf))
            ms.append((cbm * dec * dt_t[ln:ln + 1, :]).astype(BF16))
            xbd.append(jnp.where(head_of_lane == k, xs_b, jnp.zeros_like(xs_b)))
        yd = jnp.dot(jnp.concatenate(ms, axis=1), jnp.concatenate(xbd, axis=0),
                     preferred_element_type=F32)
        yg = yd + yoff * eac_c[:, gcols]
        if reverse:
            yg = yg + yf_ref[0, :, gcols]
        else:
            yg = yg + dsk_ref[:, gcols] * xs_f
        y_ref[0, :, gcols] = yg
        xd = (xs_f * wst_c[:, gcols]).astype(BF16)
        st[g] = (sg * eac_c[last_row:last_row + 1, gcols]
                 + jnp.dot(bgt, xd, preferred_element_type=F32))

    @pl.when(step == nc - 1)
    def _():
        hout_ref[0] = st[...]


def _head_expand(inner, lane0):
    e = np.zeros((LANES, inner), np.float32)
    for h in range(inner // HEAD_DIM):
        e[lane0 + h, h * HEAD_DIM:(h + 1) * HEAD_DIM] = 1.0
    return jnp.asarray(e, BF16)


def _ssd_forward(xbc, dtr, conv_w8, conv_b, dtb_row, a_row, h0, dsk, *, inner):
    b, l, xw_cols = xbc.shape
    nc = l // CHUNK
    hb = CHUNK // SUBLANES
    nhb = l // SUBLANES
    gp = inner // SSD_GROUPS
    cur = lambda width: pl.BlockSpec((1, CHUNK, width), lambda i, j: (i, j, 0))
    state = pl.BlockSpec((1, SSD_GROUPS, SSD_STATE, gp), lambda i, j: (i, 0, 0, 0))
    kern = functools.partial(_ssd_kernel, reverse=False, nc=nc, inner=inner)
    return pl.pallas_call(
        kern,
        grid=(b, nc),
        in_specs=[
            cur(xw_cols),
            pl.BlockSpec((1, SUBLANES, xw_cols), lambda i, j: (i, jnp.maximum(j * hb - 1, 0), 0)),
            pl.BlockSpec((1, SUBLANES, xw_cols), lambda i, j: (i, jnp.minimum(j * hb + hb, nhb - 1), 0)),
            cur(LANES),
            _resident((SUBLANES, xw_cols)), _resident((1, xw_cols)), _resident((1, LANES)),
            _resident((1, LANES)), _resident((LANES, inner)), state, _resident((1, inner)),
        ],
        out_specs=[cur(inner), cur(xw_cols), state],
        out_shape=[jax.ShapeDtypeStruct((b, l, inner), F32),
                   jax.ShapeDtypeStruct((b, l, xw_cols), BF16),
                   jax.ShapeDtypeStruct((b, SSD_GROUPS, SSD_STATE, gp), F32)],
        scratch_shapes=[
            pltpu.VMEM((CHUNK + 2 * SUBLANES, xw_cols), F32),
            pltpu.VMEM((CHUNK, xw_cols), F32),
            pltpu.VMEM((SSD_GROUPS, SSD_STATE, gp), F32),
        ],
        compiler_params=_params("parallel", "arbitrary"),
        name="ssd_fwd",
    )(xbc, xbc, xbc, dtr, conv_w8, conv_b, dtb_row, a_row, _head_expand(inner, 0), h0, dsk)


def _ssd_backward(xab, dtr, dtb_row, a_row, h0, yf, *, inner):
    b, l, xw_cols = xab.shape
    nc = l // CHUNK
    gp = inner // SSD_GROUPS
    cur = lambda width: pl.BlockSpec((1, CHUNK, width), lambda i, j: (i, nc - 1 - j, 0))
    state = pl.BlockSpec((1, SSD_GROUPS, SSD_STATE, gp), lambda i, j: (i, 0, 0, 0))
    kern = functools.partial(_ssd_kernel, reverse=True, nc=nc, inner=inner)
    return pl.pallas_call(
        kern,
        grid=(b, nc),
        in_specs=[cur(xw_cols), cur(LANES), _resident((1, LANES)), _resident((1, LANES)),
                  _resident((LANES, inner)), state, cur(inner)],
        out_specs=[cur(inner), state],
        out_shape=[jax.ShapeDtypeStruct((b, l, inner), F32),
                   jax.ShapeDtypeStruct((b, SSD_GROUPS, SSD_STATE, gp), F32)],
        scratch_shapes=[pltpu.VMEM((SSD_GROUPS, SSD_STATE, gp), F32)],
        compiler_params=_params("parallel", "arbitrary"),
        name="ssd_bwd",
    )(xab, dtr, dtb_row, a_row, _head_expand(inner, inner // HEAD_DIM), h0, yf)


def _pool_kernel(pp_ref, band_ref, cinv_ref, inv_ref, pw_ref, ps_ref, o_ref, cp, *, width, rows,
                 slab):
    g = pl.program_id(1)
    pad = SUBLANES * width
    l = rows * width
    rows_per_slab = slab // width

    def body(window):
        lo = window // 2
        hi = window - lo
        cp[0:pad, :] = jnp.zeros((pad, LANES), F32)
        cp[pad + l:, :] = jnp.zeros((pad, LANES), F32)

        def col_pool(s, carry):
            off = pl.multiple_of(s * slab, slab)
            v = pp_ref[0, pl.ds(off, slab), :]
            v_hi = v.astype(BF16)
            v_lo = (v - v_hi.astype(F32)).astype(BF16)
            sums = jnp.dot(band_ref[0], jnp.concatenate([v_hi, v_lo], axis=1),
                           preferred_element_type=F32)
            cp[pl.ds(pad + off, slab), :] = (sums[:, :LANES] + sums[:, LANES:]) * cinv_ref[0]
            return carry

        lax.fori_loop(0, l // slab, col_pool, 0, unroll=min(4, l // slab))

        def row_pool(s, carry):
            off = pl.multiple_of(s * slab, slab)
            parts = []
            for r in range(rows_per_slab):
                base = pad + off + r * width
                acc = cp[pl.ds(base - lo * width, width), :]
                for k in range(1 - lo, hi):
                    acc = acc + cp[pl.ds(base + k * width, width), :]
                parts.append(acc * inv_ref[0, pl.ds(s * rows_per_slab + r, 1), :])
            pooled = parts[0] if len(parts) == 1 else jnp.concatenate(parts, axis=0)
            d = (pooled - pp_ref[0, pl.ds(off, slab), :]).astype(BF16)
            y = jnp.dot(d, pw_ref[0], preferred_element_type=F32) * ps_ref[0]
            o_ref[0, pl.ds(off, slab), :] = y.astype(BF16)
            return carry

        lax.fori_loop(0, l // slab, row_pool, 0, unroll=min(4, l // slab))

    for gi, window in enumerate(POOL_WINDOWS):
        pl.when(g == gi)(functools.partial(body, window))


def _pool_tables(l, width):
    rows = l // width
    slab = max(CHUNK, width)
    n_g = len(POOL_WINDOWS)
    band = np.zeros((n_g, slab, slab), np.float32)
    cinv = np.ones((n_g, slab, LANES), np.float32)
    inv = np.ones((n_g, max(rows, SUBLANES), LANES), np.float32)
    for gi, w in enumerate(POOL_WINDOWS):
        lo, hi = w // 2, w - w // 2
        for t in range(slab):
            r, c = divmod(t, width)
            c0, c1 = max(c - lo, 0), min(c + hi, width)
            band[gi, t, r * width + c0:r * width + c1] = 1.0
            cinv[gi, t, :] = 1.0 / (c1 - c0)
        for r in range(rows):
            inv[gi, r, :] = 1.0 / (min(r + hi, rows) - max(r - lo, 0))
    return jnp.asarray(band, BF16), jnp.asarray(cinv), jnp.asarray(inv), rows, slab


def _pool_branch(pp, pool_w, pool_scale, *, width):
    b, l, pw = pp.shape
    n_g = len(POOL_WINDOWS)
    band, cinv, inv, rows, slab = _pool_tables(l, width)
    kern = functools.partial(_pool_kernel, width=width, rows=rows, slab=slab)
    return pl.pallas_call(
        kern,
        grid=(b, n_g),
        in_specs=[
            pl.BlockSpec((1, l, LANES), lambda i, g: (i, 0, g)),
            pl.BlockSpec((1, slab, slab), lambda i, g: (g, 0, 0)),
            pl.BlockSpec((1, slab, LANES), lambda i, g: (g, 0, 0)),
            pl.BlockSpec((1,) + inv.shape[1:], lambda i, g: (g, 0, 0)),
            pl.BlockSpec((1, LANES, LANES), lambda i, g: (g, 0, 0)),
            pl.BlockSpec((1, 1, LANES), lambda i, g: (g, 0, 0)),
        ],
        out_specs=pl.BlockSpec((1, l, LANES), lambda i, g: (i, 0, g)),
        out_shape=jax.ShapeDtypeStruct((b, l, pw), BF16),
        scratch_shapes=[pltpu.VMEM((l + 2 * SUBLANES * width, LANES), F32)],
        compiler_params=_params("parallel", "parallel"),
        name="pool",
    )(pp, band, cinv, inv, pool_w, pool_scale.reshape(n_g, 1, LANES))


def _mix_kernel(x_ref, mod_ref, g1_ref, g2_ref, wz_ref, wg_ref, y_ref, sg_ref, yp_ref, ygm_ref,
                wbs_ref, wbp_ref, wbg_ref, wo_ref, wr_ref, br_ref, x1_ref, h2_ref):
    x = x_ref[0]
    d = x.shape[1]
    h = _modnorm(x, g1_ref[...], mod_ref[0, 0:1, :], mod_ref[0, 1:2, :]).astype(BF16)
    z = jnp.dot(h, wz_ref[...], preferred_element_type=F32)
    t = y_ref[0] * _silu(z)
    ys = t * lax.rsqrt(jnp.mean(t * t, axis=-1, keepdims=True) + SSD_NORM_EPS) * sg_ref[...]
    gates = _sigmoid(jnp.dot(h, wg_ref[...], preferred_element_type=F32))
    merged = gates[:, :d] * jnp.dot(ys.astype(BF16), wbs_ref[...], preferred_element_type=F32)
    merged = merged + gates[:, d:2 * d] * jnp.dot(yp_ref[0], wbp_ref[...], preferred_element_type=F32)
    merged = merged + gates[:, 2 * d:] * jnp.dot(ygm_ref[0], wbg_ref[...], preferred_element_type=F32)
    mix = jnp.dot(merged.astype(BF16), wo_ref[...], preferred_element_type=F32)
    x1 = x + mod_ref[0, 2:3, :] * mix
    x1_ref[0] = x1
    h2 = _modnorm(x1, g2_ref[...], mod_ref[0, 3:4, :], mod_ref[0, 4:5, :])
    h2_ref[0, :, :d] = h2

    h2_hi = h2.astype(BF16)
    h2_lo = (h2 - h2_hi.astype(F32)).astype(BF16)
    p_hi = jnp.dot(h2_hi, wr_ref[...], preferred_element_type=F32)
    p_lo = jnp.dot(h2_lo, wr_ref[:, :LANES], preferred_element_type=F32)
    lg = p_hi[:, :LANES] + p_hi[:, LANES:] + p_lo + br_ref[...]
    lane = lax.broadcasted_iota(jnp.int32, lg.shape, 1).astype(F32)
    ninf = -jnp.inf
    big = float(LANES)
    gl = jnp.where(lane < MOE_GROUPS, lg, ninf)
    gmax = jnp.max(gl, axis=-1, keepdims=True)
    pg_top = 1.0 / jnp.sum(jnp.exp(gl - gmax), axis=-1, keepdims=True)
    gidx = jnp.min(jnp.where(gl == gmax, lane, big), axis=-1, keepdims=True)
    e0 = MOE_GROUPS + EXPERTS_PER_GROUP * gidx
    el = jnp.where((lane >= e0) & (lane < e0 + EXPERTS_PER_GROUP), lg, ninf)
    v1 = jnp.max(el, axis=-1, keepdims=True)
    i1 = jnp.min(jnp.where(el == v1, lane, big), axis=-1, keepdims=True)
    el2 = jnp.where(lane == i1, ninf, el)
    v2 = jnp.max(el2, axis=-1, keepdims=True)
    i2 = jnp.min(jnp.where(el2 == v2, lane, big), axis=-1, keepdims=True)
    r = jnp.exp(v2 - v1)
    wt1 = pg_top * (1.0 / (1.0 + r))
    wt2 = pg_top * (r / (1.0 + r))
    first_lo = i1 < i2
    lo = jnp.where(first_lo, i1, i2) - e0
    hi = jnp.where(first_lo, i2, i1) - e0
    w_lo = jnp.where(first_lo, wt1, wt2)
    w_hi = jnp.where(first_lo, wt2, wt1)
    cls = gidx * N_PAIRS + lo * (7.0 - lo) * 0.5 + (hi - lo - 1.0)
    info = jnp.where(lane == cls, 1.0, 0.0)
    info = jnp.where(lane == LANE_W_LO, w_lo, info)
    info = jnp.where(lane == LANE_W_HI, w_hi, info)
    h2_ref[0, :, d:] = info


def _mix(x, modv, g1, g2, wz, wg, y, ssd_g, yp, ygm, wbs, wbp, wbg, wo, wr, br):
    b, l, d = x.shape
    tm = min(MATMUL_TILE, l)
    tok = lambda width: pl.BlockSpec((1, tm, width), lambda i, j: (i, j, 0))
    full = lambda a: _resident(a.shape)
    return pl.pallas_call(
        _mix_kernel,
        grid=(b, l // tm),
        in_specs=[
            tok(d), pl.BlockSpec((1, SUBLANES, d), lambda i, j: (i, 0, 0)),
            full(g1), full(g2), full(wz), full(wg), tok(y.shape[2]), full(ssd_g),
            tok(yp.shape[2]), tok(ygm.shape[2]), full(wbs), full(wbp), full(wbg), full(wo),
            full(wr), full(br),
        ],
        out_specs=[tok(d), tok(d + LANES)],
        out_shape=[jax.ShapeDtypeStruct((b, l, d), F32),
                   jax.ShapeDtypeStruct((b, l, d + LANES), F32)],
        compiler_params=_params("parallel", "parallel"),
        name="mix",
    )(x, modv, g1, g2, wz, wg, y, ssd_g, yp, ygm, wbs, wbp, wbg, wo, wr, br)


def _sort_kernel(info_ref, pos_ref, tcls_ref, cnt, base, *, tile, n_slabs, n_tile_lanes):
    phase = pl.program_id(0)
    i = pl.program_id(1)
    ts = info_ref.shape[0]
    sub = lax.broadcasted_iota(jnp.int32, (LANES, LANES), 0)
    lan = lax.broadcasted_iota(jnp.int32, (LANES, LANES), 1)

    oh_t = jnp.concatenate(
        [info_ref[k * LANES:(k + 1) * LANES, :].T for k in range(n_slabs)], axis=1)
    csub = lax.broadcasted_iota(jnp.int32, oh_t.shape, 0)
    oh_t = jnp.where(csub < N_CLASSES, oh_t, 0.0)
    tile_count = jnp.sum(oh_t, axis=-1, keepdims=True)

    @pl.when((phase == 0) & (i == 0))
    def _():
        cnt[...] = jnp.zeros_like(cnt)

    @pl.when(phase == 0)
    def _():
        cnt[...] += jnp.broadcast_to(tile_count, cnt.shape)

    @pl.when((phase == 1) & (i == 0))
    def _():
        padded = jnp.ceil(cnt[...] * (1.0 / tile)) * tile
        starts = jnp.dot((lan < sub).astype(F32), padded, precision=HIGHEST,
                         preferred_element_type=F32)
        base[...] = starts
        tcls_ref[...] = jnp.zeros_like(tcls_ref)

    @pl.when(phase == 1)
    def _():
        ti = lax.broadcasted_iota(jnp.int32, (ts, ts), 0)
        tj = lax.broadcasted_iota(jnp.int32, (ts, ts), 1)
        before = jnp.dot(oh_t.astype(BF16), (ti < tj).astype(BF16), preferred_element_type=F32)
        posv = jnp.sum(oh_t * (before + base[:, 0:1]), axis=0, keepdims=True)
        pos_ref[0] = posv.astype(jnp.int32)
        padded = jnp.ceil(cnt[...] * (1.0 / tile)) * tile
        starts = jnp.dot((lan < sub).astype(F32), padded, precision=HIGHEST,
                         preferred_element_type=F32)
        ends = (starts + padded)[:, 0:1]
        tl = lax.broadcasted_iota(jnp.int32, (LANES, n_tile_lanes), 1).astype(F32) * tile
        cs = lax.broadcasted_iota(jnp.int32, (LANES, n_tile_lanes), 0)
        done = jnp.where((cs < N_CLASSES) & (ends <= tl), 1.0, 0.0)
        tc = jnp.minimum(jnp.sum(done, axis=0, keepdims=True), N_CLASSES - 1.0)
        total = jnp.max(jnp.where(cs < N_CLASSES, ends, 0.0), axis=0, keepdims=True)
        valid = jnp.where(tl[0:1, :] < total, 1.0, 0.0)
        tcls_ref[0, 0:1, :] = tc.astype(jnp.int32)
        tcls_ref[0, 1:2, :] = valid.astype(jnp.int32)
        base[...] += jnp.broadcast_to(tile_count, base.shape)


def _route_sort(h2ext, *, d, n_exp_tiles):
    t = h2ext.shape[0]
    ts = min(1024, t)
    nt = t // ts
    n_tile_lanes = -(-n_exp_tiles // LANES) * LANES
    kern = functools.partial(_sort_kernel, tile=EXPERT_TILE, n_slabs=ts // LANES,
                             n_tile_lanes=n_tile_lanes)
    pos, tcls = pl.pallas_call(
        kern,
        grid=(2, nt),
        in_specs=[pl.BlockSpec((ts, LANES), lambda p, i: (i, d // LANES))],
        out_specs=[pl.BlockSpec((1, 1, ts), lambda p, i: (i * p, 0, 0)),
                   pl.BlockSpec((1, SUBLANES, n_tile_lanes), lambda p, i: (0, 0, 0))],
        out_shape=[jax.ShapeDtypeStruct((nt, 1, ts), jnp.int32),
                   jax.ShapeDtypeStruct((1, SUBLANES, n_tile_lanes), jnp.int32)],
        scratch_shapes=[pltpu.VMEM((LANES, LANES), F32), pltpu.VMEM((LANES, LANES), F32)],
        compiler_params=_params("arbitrary", "arbitrary"),
        name="route_sort",
    )(h2ext)
    return pos.reshape(t), tcls[0, 0, :n_exp_tiles], tcls[0, 1, :n_exp_tiles]


def _dispatch_kernel(pos_ref, rows_ref, init_ref, out_ref, sem):
    del init_ref
    n = rows_ref.shape[0]

    def issue(r, carry):
        pltpu.make_async_copy(rows_ref.at[pl.ds(r, 1)], out_ref.at[pl.ds(pos_ref[0, 0, r], 1)],
                              sem).start()
        return carry

    lax.fori_loop(0, n, issue, 0, unroll=8)
    pltpu.make_async_copy(rows_ref, out_ref.at[pl.ds(0, n)], sem).wait()


def _dispatch(h2ext, pos, *, n_sorted_rows):
    t, w = h2ext.shape
    tm = TOKEN_TILE
    nt = t // tm
    return pl.pallas_call(
        _dispatch_kernel,
        grid=(nt,),
        in_specs=[
            pl.BlockSpec((1, 1, tm), lambda i: (i, 0, 0), memory_space=pltpu.SMEM),
            pl.BlockSpec((tm, w), lambda i: (i, 0)),
            pl.BlockSpec(memory_space=pl.ANY),
        ],
        out_specs=pl.BlockSpec(memory_space=pl.ANY),
        out_shape=jax.ShapeDtypeStruct((n_sorted_rows, w), F32),
        scratch_shapes=[pltpu.SemaphoreType.DMA],
        input_output_aliases={2: 0},
        compiler_params=_params("arbitrary"),
        name="dispatch",
    )(pos.reshape(nt, 1, tm), h2ext, jnp.zeros((n_sorted_rows, w), F32))


def _expert_kernel(elo_ref, ehi_ref, valid_ref, rows_ref, wil_ref, wih_ref, wol_ref, woh_ref,
                   o_ref, *, d, d_expert):
    del elo_ref, ehi_ref
    i = pl.program_id(0)

    @pl.when(valid_ref[i] == 0)
    def _():
        o_ref[...] = jnp.zeros_like(o_ref)

    @pl.when(valid_ref[i] != 0)
    def _():
        h = rows_ref[:, :d].astype(BF16)
        info = rows_ref[:, d:]

        def expert(wi_ref, wo_ref):
            gu = jnp.dot(h, wi_ref[0], preferred_element_type=F32)
            hid = _silu(gu[:, :d_expert]) * gu[:, d_expert:]
            return jnp.dot(hid.astype(BF16), wo_ref[0], preferred_element_type=F32)

        o_ref[...] = (info[:, LANE_W_LO:LANE_W_LO + 1] * expert(wil_ref, wol_ref)
                      + info[:, LANE_W_HI:LANE_W_HI + 1] * expert(wih_ref, woh_ref))


def _experts(sorted_rows, e_lo, e_hi, valid, w_e_in, w_e_out, *, d):
    n_rows, w = sorted_rows.shape
    tm = EXPERT_TILE
    n_tiles = n_rows // tm
    d_expert = w_e_out.shape[1]
    kern = functools.partial(_expert_kernel, d=d, d_expert=d_expert)
    grid_spec = pltpu.PrefetchScalarGridSpec(
        num_scalar_prefetch=3,
        grid=(n_tiles,),
        in_specs=[
            pl.BlockSpec((tm, w), lambda i, lo, hi, v: (i, 0)),
            pl.BlockSpec((1,) + w_e_in.shape[1:], lambda i, lo, hi, v: (lo[i], 0, 0)),
            pl.BlockSpec((1,) + w_e_in.shape[1:], lambda i, lo, hi, v: (hi[i], 0, 0)),
            pl.BlockSpec((1,) + w_e_out.shape[1:], lambda i, lo, hi, v: (lo[i], 0, 0)),
            pl.BlockSpec((1,) + w_e_out.shape[1:], lambda i, lo, hi, v: (hi[i], 0, 0)),
        ],
        out_specs=pl.BlockSpec((tm, d), lambda i, lo, hi, v: (i, 0)),
    )
    return pl.pallas_call(
        kern,
        grid_spec=grid_spec,
        out_shape=jax.ShapeDtypeStruct((n_rows, d), F32),
        compiler_params=_params("arbitrary"),
        name="experts",
    )(e_lo, e_hi, valid, sorted_rows, w_e_in, w_e_in, w_e_out, w_e_out)


def _combine_kernel(pos_ref, x1_ref, mod_ref, gf_ref, moe_ref, o_ref, rows, sem, *, final):
    n = rows.shape[0]

    def issue(r, carry):
        pltpu.make_async_copy(moe_ref.at[pl.ds(pos_ref[0, 0, r], 1)], rows.at[pl.ds(r, 1)],
                              sem).start()
        return carry

    lax.fori_loop(0, n, issue, 0, unroll=8)
    pltpu.make_async_copy(moe_ref.at[pl.ds(0, n)], rows, sem).wait()
    x2 = x1_ref[...] + mod_ref[0, 5:6, :] * rows[...]
    if final:
        x2 = x2 * lax.rsqrt(jnp.mean(x2 * x2, axis=-1, keepdims=True) + RMS_EPS) * gf_ref[...]
    o_ref[...] = x2


def _combine(x1, pos, modv, g_final, moe_sorted, *, seq_len, final):
    t, d = x1.shape
    tm = TOKEN_TILE
    nt = t // tm
    tiles_per_seq = seq_len // tm
    kern = functools.partial(_combine_kernel, final=final)
    return pl.pallas_call(
        kern,
        grid=(nt,),
        in_specs=[
            pl.BlockSpec((1, 1, tm), lambda i: (i, 0, 0), memory_space=pltpu.SMEM),
            pl.BlockSpec((tm, d), lambda i: (i, 0)),
            pl.BlockSpec((1, SUBLANES, d), lambda i: (i // tiles_per_seq, 0, 0)),
            pl.BlockSpec((1, d), lambda i: (0, 0)),
            pl.BlockSpec(memory_space=pl.ANY),
        ],
        out_specs=pl.BlockSpec((tm, d), lambda i: (i, 0)),
        out_shape=jax.ShapeDtypeStruct((t, d), F32),
        scratch_shapes=[pltpu.VMEM((tm, d), F32), pltpu.SemaphoreType.DMA],
        compiler_params=_params("arbitrary"),
        name="combine",
    )(pos.reshape(nt, 1, tm), x1, modv, g_final, moe_sorted)


def _pair_tables():
    lo, hi = [], []
    for g in range(MOE_GROUPS):
        for a in range(EXPERTS_PER_GROUP):
            for b in range(a + 1, EXPERTS_PER_GROUP):
                lo.append(g * EXPERTS_PER_GROUP + a)
                hi.append(g * EXPERTS_PER_GROUP + b)
    return jnp.asarray(lo, jnp.int32), jnp.asarray(hi, jnp.int32)


def _stream_layer(x, modv, lw, h0f, h0b, *, width, final, g_final, need_tokens=True):
    b, l, d = x.shape
    inner = lw["inner"]
    xbc, dtr, pp, ygm = _inproj(x, modv, lw["g1"], lw["w_ssd"], lw["ln_g"], lw["ln_b"], lw["ws"],
                                lw["bs_b"], xbc_w=lw["xbc_w"], pool_w=lw["pool_cols"],
                                gmlp_w=lw["gmlp_w"])
    yf, xab, hf = _ssd_forward(xbc, dtr, lw["conv_w8"], lw["conv_b"], lw["dtb_row"], lw["a_row"],
                               h0f, lw["dsk"], inner=inner)
    y, hb = _ssd_backward(xab, dtr, lw["dtb_row"], lw["a_row"], h0b, yf, inner=inner)
    if not need_tokens:
        return None, hf, hb
    yp = _pool_branch(pp, lw["pool_w"], lw["pool_scale"], width=width)
    x1, h2ext = _mix(x, modv, lw["g1"], lw["g2"], lw["wz"], lw["wg"], y, lw["ssd_g"], yp, ygm,
                     lw["wbs"], lw["wbp"], lw["wbg"], lw["wo"], lw["wr"], lw["br"])
    t = b * l
    h2ext = h2ext.reshape(t, d + LANES)
    n_exp_tiles = t // EXPERT_TILE + N_CLASSES
    pos, tcls, valid = _route_sort(h2ext, d=d, n_exp_tiles=n_exp_tiles)
    pair_lo, pair_hi = _pair_tables()
    sorted_rows = _dispatch(h2ext, pos, n_sorted_rows=n_exp_tiles * EXPERT_TILE)
    moe_sorted = _experts(sorted_rows, pair_lo[tcls], pair_hi[tcls], valid, lw["w_e_in"],
                          lw["w_e_out"], d=d)
    x2 = _combine(x1.reshape(t, d), pos, modv, g_final, moe_sorted, seq_len=l, final=final)
    return x2.reshape(b, l, d), hf, hb


def kernel(x, c, ctx, c_ctx, w_mod, b_mod, g_norm1, g_norm2, w_in, conv_w, conv_b, dt_bias, a_log,
           d_skip, ssd_norm_g, pool_w, pool_scale, gmlp_ln_g, gmlp_ln_b, gmlp_ws, gmlp_bs, w_br_ssd,
           w_br_pool, w_br_gmlp, w_o, w_rg, b_rg, w_re, b_re, w_e_in, w_e_out, g_final):
    b, seq, d = x.shape
    depth = w_mod.shape[0]
    n_heads = d_skip.shape[1]
    inner = n_heads * HEAD_DIM
    bcw = SSD_GROUPS * SSD_STATE
    xbc_w = inner + 2 * bcw
    pool_cols = pool_w.shape[1] * pool_w.shape[2]
    gmlp_w = gmlp_ln_g.shape[1]
    off_xbc = inner
    off_dt = off_xbc + xbc_w
    off_pool = off_dt + 2 * n_heads
    off_gmlp = off_pool + pool_cols
    off_gate = off_gmlp + 2 * gmlp_w

    rows = jnp.zeros((SUBLANES, d), F32).at[:b].set(c).at[b].set(c_ctx)
    mod = _modulation(rows, w_mod, b_mod)
    gp = inner // SSD_GROUPS
    zero_state = jnp.zeros((b, SSD_GROUPS, SSD_STATE, gp), F32)
    g_final_row = g_final.reshape(1, d)

    def pad_rows(m):
        return jnp.pad(m.reshape(m.shape[0], 6, d), ((0, 0), (0, 2), (0, 0)))

    for i in range(depth):
        last = i == depth - 1
        mod_lat = pad_rows(mod[i, :b])
        mod_ctx = pad_rows(jnp.broadcast_to(mod[i, b:b + 1], (b, 6 * d)))
        wi = w_in[i]
        dt_pad = jnp.zeros((d, LANES - 2 * n_heads), F32)
        a = -jnp.exp(a_log[i].astype(F32)).reshape(1, 2 * n_heads)
        lane_pad = ((0, 0), (0, LANES - 2 * n_heads))
        wr = jnp.pad(jnp.concatenate([w_rg[i], w_re[i]], axis=1),
                     ((0, 0), (0, LANES - MOE_GROUPS - MOE_GROUPS * EXPERTS_PER_GROUP)))
        br = jnp.pad(jnp.concatenate([b_rg[i], b_re[i]]).reshape(1, -1),
                     ((0, 0), (0, LANES - MOE_GROUPS - MOE_GROUPS * EXPERTS_PER_GROUP)))
        lw = dict(
            inner=inner, xbc_w=xbc_w, pool_cols=pool_cols, gmlp_w=gmlp_w,
            g1=g_norm1[i].reshape(1, d), g2=g_norm2[i].reshape(1, d),
            w_ssd=jnp.concatenate([wi[:, off_xbc:off_dt], wi[:, off_pool:off_gate],
                                   wi[:, off_dt:off_pool], dt_pad], axis=1).astype(BF16),
            wz=wi[:, :inner].astype(BF16), wg=wi[:, off_gate:].astype(BF16),
            ln_g=gmlp_ln_g[i].reshape(1, gmlp_w), ln_b=gmlp_ln_b[i].reshape(1, gmlp_w),
            ws=gmlp_ws[i].astype(BF16),
            bs_b=jnp.broadcast_to(gmlp_bs[i][:, :, None], gmlp_bs[i].shape + (LANES,)),
            conv_w8=jnp.pad(conv_w[i], ((0, SUBLANES - D_CONV), (0, 0))),
            conv_b=conv_b[i].reshape(1, xbc_w),
            dtb_row=jnp.pad(dt_bias[i].reshape(1, 2 * n_heads), lane_pad),
            a_row=jnp.pad(a, lane_pad),
            dsk=jnp.repeat(d_skip[i], HEAD_DIM).reshape(1, inner),
            ssd_g=ssd_norm_g[i].reshape(1, inner),
            pool_w=pool_w[i].astype(BF16), pool_scale=pool_scale[i],
            wbs=w_br_ssd[i].astype(BF16), wbp=w_br_pool[i].astype(BF16),
            wbg=w_br_gmlp[i].astype(BF16), wo=w_o[i].astype(BF16),
            wr=jnp.concatenate([wr.astype(BF16),
                                (wr - wr.astype(BF16).astype(F32)).astype(BF16)], axis=1),
            br=br,
            w_e_in=w_e_in[i].astype(BF16), w_e_out=w_e_out[i].astype(BF16),
        )
        ctx_new, hf_c, hb_c = _stream_layer(ctx, mod_ctx, lw, zero_state, zero_state,
                                            width=ctx.shape[1], final=False, g_final=g_final_row,
                                            need_tokens=not last)
        x, _, _ = _stream_layer(x, mod_lat, lw, hf_c, hb_c, width=GRID_W, final=last,
                                g_final=g_final_row)
        if not last:
            ctx = ctx_new
    return x
```

```python
import functools

import numpy as np
import jax
import jax.numpy as jnp
from jax import lax
from jax.experimental import pallas as pl
from jax.experimental.pallas import tpu as pltpu

F32 = jnp.float32
BF16 = jnp.bfloat16
HIGHEST = lax.Precision.HIGHEST

LANES = 128
SUBLANES = 8
VMEM_LIMIT_BYTES = 56 * 1024 * 1024

HEAD_DIM = 64
SSD_GROUPS = 4
SSD_STATE = 128
CHUNK = 128
D_CONV = 5
POOL_WINDOWS = (2, 4, 8, 16)
GMLP_GROUPS = 4
MOE_GROUPS = 4
EXPERTS_PER_GROUP = 4
N_PAIRS = 6
N_CLASSES = MOE_GROUPS * N_PAIRS
RMS_EPS = 1e-6
LN_EPS = 1e-5
SSD_NORM_EPS = 1e-5
GRID_W = 64

LANE_W_LO = 32
LANE_W_HI = 33

EXPERT_TILE = 256
ROW_TILE = 1024
ROW_UNROLL = 8
MATMUL_TILE = 512


def _params(*sem):
    return pltpu.CompilerParams(dimension_semantics=sem, vmem_limit_bytes=VMEM_LIMIT_BYTES)


def _sigmoid(v):
    return 0.5 * (1.0 + jnp.tanh(0.5 * v))


def _silu(v):
    return v * _sigmoid(v)


def _resident(shape):
    return pl.BlockSpec(shape, lambda *_: (0,) * len(shape), pipeline_mode=pl.Buffered(1))


def _modnorm(x, g, shift, scale):
    y = x * lax.rsqrt(jnp.mean(x * x, axis=-1, keepdims=True) + RMS_EPS)
    return y * g * (1.0 + scale) + shift


def _mod_kernel(c_ref, w_ref, b_ref, o_ref):
    s = _silu(c_ref[...])
    o_ref[0] = jnp.dot(s, w_ref[0], precision=HIGHEST, preferred_element_type=F32) + b_ref[0]


def _modulation(rows, w_mod, b_mod):
    depth, d, n = w_mod.shape
    tn = 1536
    return pl.pallas_call(
        _mod_kernel,
        grid=(depth, n // tn),
        in_specs=[
            pl.BlockSpec((SUBLANES, d), lambda i, j: (0, 0)),
            pl.BlockSpec((1, d, tn), lambda i, j: (i, 0, j)),
            pl.BlockSpec((1, 1, tn), lambda i, j: (i, 0, j)),
        ],
        out_specs=pl.BlockSpec((1, SUBLANES, tn), lambda i, j: (i, 0, j)),
        out_shape=jax.ShapeDtypeStruct((depth, SUBLANES, n), F32),
        compiler_params=_params("parallel", "parallel"),
        name="modulation",
    )(rows, w_mod, b_mod.reshape(depth, 1, n))


def _inproj_kernel(x_ref, xp_ref, xn_ref, mod_ref, g_ref, wx_ref, wpg_ref, wdt_ref, cw_ref, cb_ref,
                   lng_ref, lnb_ref, ws_ref, bs_ref, xab_ref, dt_ref, pp_ref, ygm_ref, xw,
                   *, pool_w, gmlp_w):
    j = pl.program_id(1)
    tm = x_ref.shape[1]

    def normed(v):
        return _modnorm(v, g_ref[...], mod_ref[0, 0:1, :], mod_ref[0, 1:2, :]).astype(BF16)

    h = normed(x_ref[0])
    pv = jnp.where(j > 0, 1.0, 0.0)
    nv = jnp.where(j < pl.num_programs(1) - 1, 1.0, 0.0)
    xc_prev = jnp.dot(normed(xp_ref[0]), wx_ref[...], preferred_element_type=F32) * pv
    xc_cur = jnp.dot(h, wx_ref[...], preferred_element_type=F32)
    xc_next = jnp.dot(normed(xn_ref[0]), wx_ref[...], preferred_element_type=F32) * nv
    for s in range(xw.shape[0]):
        cols = slice(s * LANES, (s + 1) * LANES)
        xw[s, 0:SUBLANES, :] = xc_prev[:, cols]
        xw[s, SUBLANES:SUBLANES + tm, :] = xc_cur[:, cols]
        xw[s, SUBLANES + tm:, :] = xc_next[:, cols]
    for s in range(xw.shape[0]):
        cols = slice(s * LANES, (s + 1) * LANES)
        for rb in range(tm // CHUNK):
            r0 = SUBLANES + rb * CHUNK - D_CONV // 2
            acc = cb_ref[:, cols] + cw_ref[0:1, cols] * xw[s, r0:r0 + CHUNK, :]
            for k in range(1, D_CONV):
                acc = acc + cw_ref[k:k + 1, cols] * xw[s, r0 + k:r0 + k + CHUNK, :]
            xab_ref[0, rb * CHUNK:(rb + 1) * CHUNK, cols] = _silu(acc).astype(BF16)

    dt_ref[0] = jnp.dot(h, wdt_ref[...], preferred_element_type=F32)
    pg = jnp.dot(h, wpg_ref[...], preferred_element_type=F32)
    pp_ref[0] = pg[:, :pool_w]
    uv = pg[:, pool_w:]
    act = uv * (0.5 * (1.0 + jnp.tanh(0.7978845608028654 * (uv + 0.044715 * (uv * uv * uv)))))
    u = act[:, :gmlp_w]
    v = act[:, gmlp_w:]
    mu = jnp.mean(v, axis=-1, keepdims=True)
    vc = v - mu
    var = jnp.mean(vc * vc, axis=-1, keepdims=True)
    vn = (vc * lax.rsqrt(var + LN_EPS) * lng_ref[...] + lnb_ref[...]).astype(BF16)
    gw = gmlp_w // GMLP_GROUPS
    for c in range(tm // CHUNK):
        rows = slice(c * CHUNK, (c + 1) * CHUNK)
        for g in range(GMLP_GROUPS):
            cols = slice(g * gw, (g + 1) * gw)
            mixed = jnp.dot(ws_ref[g], vn[rows, cols], preferred_element_type=F32) + bs_ref[g]
            ygm_ref[0, rows, cols] = (u[rows, cols] * mixed).astype(BF16)


def _inproj(x, modv, g_norm, wx, wpg, wdt, conv_w8, conv_b, ln_g, ln_b, ws, bs_b, *, pool_w, gmlp_w):
    b, l, d = x.shape
    tm = min(MATMUL_TILE, l)
    xbc_w = wx.shape[1]
    hb = tm // SUBLANES
    nhb = l // SUBLANES
    kern = functools.partial(_inproj_kernel, pool_w=pool_w, gmlp_w=gmlp_w)
    tok = lambda width: pl.BlockSpec((1, tm, width), lambda i, j: (i, j, 0))
    full = lambda a: _resident(a.shape)
    return pl.pallas_call(
        kern,
        grid=(b, l // tm),
        in_specs=[
            tok(d),
            pl.BlockSpec((1, SUBLANES, d), lambda i, j: (i, jnp.maximum(j * hb - 1, 0), 0)),
            pl.BlockSpec((1, SUBLANES, d), lambda i, j: (i, jnp.minimum(j * hb + hb, nhb - 1), 0)),
            pl.BlockSpec((1, SUBLANES, d), lambda i, j: (i, 0, 0)),
            full(g_norm), full(wx), full(wpg), full(wdt), full(conv_w8), full(conv_b),
            full(ln_g), full(ln_b), full(ws), full(bs_b),
        ],
        out_specs=[tok(xbc_w), tok(LANES), tok(pool_w), tok(gmlp_w)],
        out_shape=[
            jax.ShapeDtypeStruct((b, l, xbc_w), BF16),
            jax.ShapeDtypeStruct((b, l, LANES), F32),
            jax.ShapeDtypeStruct((b, l, pool_w), F32),
            jax.ShapeDtypeStruct((b, l, gmlp_w), BF16),
        ],
        scratch_shapes=[pltpu.VMEM((xbc_w // LANES, tm + 2 * SUBLANES, LANES), F32)],
        compiler_params=_params("parallel", "parallel"),
        name="inproj",
    )(x, x, x, modv, g_norm, wx, wpg, wdt, conv_w8, conv_b, ln_g, ln_b, ws, bs_b)


def _ssd_kernel(xab_ref, dt_ref, dtb_ref, a_ref, exp_ref, h0_ref, add_ref, y_ref, hout_ref, st,
                *, reverse, nc, inner):
    step = pl.program_id(1)
    bcw = SSD_GROUPS * SSD_STATE
    heads_per_group = inner // HEAD_DIM // SSD_GROUPS
    gp = heads_per_group * HEAD_DIM
    lane0 = (inner // HEAD_DIM) if reverse else 0

    @pl.when(step == 0)
    def _():
        st[...] = h0_ref[0]

    def cols_f32(lo, hi):
        return xab_ref[0, :, lo:hi].astype(F32)

    def cols_bf16(lo, hi):
        return xab_ref[0, :, lo:hi]

    z = dt_ref[0] + dtb_ref[...]
    dtm = jnp.maximum(z, 0.0) + jnp.log(1.0 + jnp.exp(-jnp.abs(z)))
    da = dtm * a_ref[...]
    ii = lax.broadcasted_iota(jnp.int32, (CHUNK, CHUNK), 0)
    jj = lax.broadcasted_iota(jnp.int32, (CHUNK, CHUNK), 1)
    causal = (ii <= jj) if reverse else (ii >= jj)
    acum = jnp.dot(causal.astype(F32), da, precision=HIGHEST, preferred_element_type=F32)
    alast = acum[0:1, :] if reverse else acum[CHUNK - 1:CHUNK, :]
    acum_t = acum.T
    dt_t = dtm.T

    def per_channel(v):
        v_hi = v.astype(BF16)
        v_lo = (v - v_hi.astype(F32)).astype(BF16)
        return (jnp.dot(v_hi, exp_ref[...], preferred_element_type=F32)
                + jnp.dot(v_lo, exp_ref[...], preferred_element_type=F32))

    eac_c = per_channel(jnp.exp(acum))
    wst_c = per_channel(dtm * jnp.exp(alast - acum))
    last_row = 0 if reverse else CHUNK - 1
    head_of_lane = lax.broadcasted_iota(jnp.int32, (CHUNK, gp), 1) // HEAD_DIM

    for g in range(SSD_GROUPS):
        b0 = inner + g * SSD_STATE
        c0 = inner + bcw + g * SSD_STATE
        gcols = slice(g * gp, (g + 1) * gp)
        cg = cols_bf16(c0, c0 + SSD_STATE)
        bgt = cols_f32(b0, b0 + SSD_STATE).T.astype(BF16)
        cbm = jnp.dot(cg, bgt, preferred_element_type=F32)
        sg = st[g]
        yoff = jnp.dot(cg, sg.astype(BF16), preferred_element_type=F32)
        xs_f = cols_f32(g * gp, (g + 1) * gp)
        xs_b = cols_bf16(g * gp, (g + 1) * gp)
        ms, xbd = [], []
        for k in range(heads_per_group):
            ln = lane0 + g * heads_per_group + k
            seg = acum[:, ln:ln + 1] - acum_t[ln:ln + 1, :]
            dec = jnp.exp(jnp.where(causal, seg, -jnp.inf))
            ms.append((cbm * dec * dt_t[ln:ln + 1, :]).astype(BF16))
            xbd.append(jnp.where(head_of_lane == k, xs_b, jnp.zeros_like(xs_b)))
        yd = jnp.dot(jnp.concatenate(ms, axis=1), jnp.concatenate(xbd, axis=0),
                     preferred_element_type=F32)
        yg = yd + yoff * eac_c[:, gcols]
        if reverse:
            yg = yg + add_ref[0, :, gcols]
        else:
            yg = yg + add_ref[:, gcols] * xs_f
        y_ref[0, :, gcols] = yg
        xd = (xs_f * wst_c[:, gcols]).astype(BF16)
        st[g] = (sg * eac_c[last_row:last_row + 1, gcols]
                 + jnp.dot(bgt, xd, preferred_element_type=F32))

    @pl.when(step == nc - 1)
    def _():
        hout_ref[0] = st[...]


def _head_expand(inner, lane0):
    e = np.zeros((LANES, inner), np.float32)
    for h in range(inner // HEAD_DIM):
        e[lane0 + h, h * HEAD_DIM:(h + 1) * HEAD_DIM] = 1.0
    return jnp.asarray(e, BF16)


def _ssd_scan(xab, dtr, dtb_row, a_row, h0, add, *, reverse, inner):
    b, l, xw_cols = xab.shape
    nc = l // CHUNK
    gp = inner // SSD_GROUPS
    chunk_of = (lambda j: nc - 1 - j) if reverse else (lambda j: j)
    cur = lambda width: pl.BlockSpec((1, CHUNK, width), lambda i, j: (i, chunk_of(j), 0))
    state = pl.BlockSpec((1, SSD_GROUPS, SSD_STATE, gp), lambda i, j: (i, 0, 0, 0))
    lane0 = (inner // HEAD_DIM) if reverse else 0
    kern = functools.partial(_ssd_kernel, reverse=reverse, nc=nc, inner=inner)
    return pl.pallas_call(
        kern,
        grid=(b, nc),
        in_specs=[cur(xw_cols), cur(LANES), _resident((1, LANES)), _resident((1, LANES)),
                  _resident((LANES, inner)), state,
                  cur(inner) if reverse else _resident((1, inner))],
        out_specs=[cur(inner), state],
        out_shape=[jax.ShapeDtypeStruct((b, l, inner), F32),
                   jax.ShapeDtypeStruct((b, SSD_GROUPS, SSD_STATE, gp), F32)],
        scratch_shapes=[pltpu.VMEM((SSD_GROUPS, SSD_STATE, gp), F32)],
        compiler_params=_params("parallel", "arbitrary"),
        name="ssd_bwd" if reverse else "ssd_fwd",
    )(xab, dtr, dtb_row, a_row, _head_expand(inner, lane0), h0, add)


def _pool_kernel(pp_ref, band_ref, cinv_ref, inv_ref, pw_ref, ps_ref, o_ref, cp, *, width, rows,
                 slab):
    g = pl.program_id(1)
    pad = SUBLANES * width
    l = rows * width
    rows_per_slab = slab // width

    def body(window):
        lo = window // 2
        hi = window - lo
        cp[0:pad, :] = jnp.zeros((pad, LANES), F32)
        cp[pad + l:, :] = jnp.zeros((pad, LANES), F32)

        def col_pool(s, carry):
            off = pl.multiple_of(s * slab, slab)
            v = pp_ref[0, pl.ds(off, slab), :]
            v_hi = v.astype(BF16)
            v_lo = (v - v_hi.astype(F32)).astype(BF16)
            sums = jnp.dot(band_ref[0], jnp.concatenate([v_hi, v_lo], axis=1),
                           preferred_element_type=F32)
            cp[pl.ds(pad + off, slab), :] = (sums[:, :LANES] + sums[:, LANES:]) * cinv_ref[0]
            return carry

        lax.fori_loop(0, l // slab, col_pool, 0, unroll=min(4, l // slab))

        def row_pool(s, carry):
            off = pl.multiple_of(s * slab, slab)
            parts = []
            for r in range(rows_per_slab):
                base = pad + off + r * width
                acc = cp[pl.ds(base - lo * width, width), :]
                for k in range(1 - lo, hi):
                    acc = acc + cp[pl.ds(base + k * width, width), :]
                parts.append(acc * inv_ref[0, pl.ds(s * rows_per_slab + r, 1), :])
            pooled = parts[0] if len(parts) == 1 else jnp.concatenate(parts, axis=0)
            d = (pooled - pp_ref[0, pl.ds(off, slab), :]).astype(BF16)
            y = jnp.dot(d, pw_ref[0], preferred_element_type=F32) * ps_ref[0]
            o_ref[0, pl.ds(off, slab), :] = y.astype(BF16)
            return carry

        lax.fori_loop(0, l // slab, row_pool, 0, unroll=min(4, l // slab))

    for gi, window in enumerate(POOL_WINDOWS):
        pl.when(g == gi)(functools.partial(body, window))


def _pool_tables(l, width):
    rows = l // width
    slab = max(CHUNK, width)
    n_g = len(POOL_WINDOWS)
    band = np.zeros((n_g, slab, slab), np.float32)
    cinv = np.ones((n_g, slab, LANES), np.float32)
    inv = np.ones((n_g, max(rows, SUBLANES), LANES), np.float32)
    for gi, w in enumerate(POOL_WINDOWS):
        lo, hi = w // 2, w - w // 2
        for t in range(slab):
            r, c = divmod(t, width)
            c0, c1 = max(c - lo, 0), min(c + hi, width)
            band[gi, t, r * width + c0:r * width + c1] = 1.0
            cinv[gi, t, :] = 1.0 / (c1 - c0)
        for r in range(rows):
            inv[gi, r, :] = 1.0 / (min(r + hi, rows) - max(r - lo, 0))
    return jnp.asarray(band, BF16), jnp.asarray(cinv), jnp.asarray(inv), rows, slab


def _pool_branch(pp, pool_w, pool_scale, *, width):
    b, l, pw = pp.shape
    n_g = len(POOL_WINDOWS)
    band, cinv, inv, rows, slab = _pool_tables(l, width)
    kern = functools.partial(_pool_kernel, width=width, rows=rows, slab=slab)
    return pl.pallas_call(
        kern,
        grid=(b, n_g),
        in_specs=[
            pl.BlockSpec((1, l, LANES), lambda i, g: (i, 0, g)),
            pl.BlockSpec((1, slab, slab), lambda i, g: (g, 0, 0)),
            pl.BlockSpec((1, slab, LANES), lambda i, g: (g, 0, 0)),
            pl.BlockSpec((1,) + inv.shape[1:], lambda i, g: (g, 0, 0)),
            pl.BlockSpec((1, LANES, LANES), lambda i, g: (g, 0, 0)),
            pl.BlockSpec((1, 1, LANES), lambda i, g: (g, 0, 0)),
        ],
        out_specs=pl.BlockSpec((1, l, LANES), lambda i, g: (i, 0, g)),
        out_shape=jax.ShapeDtypeStruct((b, l, pw), BF16),
        scratch_shapes=[pltpu.VMEM((l + 2 * SUBLANES * width, LANES), F32)],
        compiler_params=_params("parallel", "parallel"),
        name="pool",
    )(pp, band, cinv, inv, pool_w, pool_scale.reshape(n_g, 1, LANES))


def _mix_kernel(x_ref, mod_ref, g1_ref, g2_ref, wz_ref, wg_ref, y_ref, sg_ref, yp_ref, ygm_ref,
                wbs_ref, wbp_ref, wbg_ref, wo_ref, wr_ref, br_ref, x1_ref, h2_ref):
    x = x_ref[0]
    d = x.shape[1]
    h = _modnorm(x, g1_ref[...], mod_ref[0, 0:1, :], mod_ref[0, 1:2, :]).astype(BF16)
    z = jnp.dot(h, wz_ref[...], preferred_element_type=F32)
    t = y_ref[0] * _silu(z)
    ys = t * lax.rsqrt(jnp.mean(t * t, axis=-1, keepdims=True) + SSD_NORM_EPS) * sg_ref[...]
    gates = _sigmoid(jnp.dot(h, wg_ref[...], preferred_element_type=F32))
    merged = gates[:, :d] * jnp.dot(ys.astype(BF16), wbs_ref[...], preferred_element_type=F32)
    merged = merged + gates[:, d:2 * d] * jnp.dot(yp_ref[0], wbp_ref[...], preferred_element_type=F32)
    merged = merged + gates[:, 2 * d:] * jnp.dot(ygm_ref[0], wbg_ref[...], preferred_element_type=F32)
    mix = jnp.dot(merged.astype(BF16), wo_ref[...], preferred_element_type=F32)
    x1 = x + mod_ref[0, 2:3, :] * mix
    x1_ref[0] = x1
    h2 = _modnorm(x1, g2_ref[...], mod_ref[0, 3:4, :], mod_ref[0, 4:5, :])
    h2_ref[0, :, :d] = h2

    h2_hi = h2.astype(BF16)
    h2_lo = (h2 - h2_hi.astype(F32)).astype(BF16)
    p_hi = jnp.dot(h2_hi, wr_ref[...], preferred_element_type=F32)
    p_lo = jnp.dot(h2_lo, wr_ref[:, :LANES], preferred_element_type=F32)
    lg = p_hi[:, :LANES] + p_hi[:, LANES:] + p_lo + br_ref[...]
    lane = lax.broadcasted_iota(jnp.int32, lg.shape, 1).astype(F32)
    ninf = -jnp.inf
    big = float(LANES)
    gl = jnp.where(lane < MOE_GROUPS, lg, ninf)
    gmax = jnp.max(gl, axis=-1, keepdims=True)
    pg_top = 1.0 / jnp.sum(jnp.exp(gl - gmax), axis=-1, keepdims=True)
    gidx = jnp.min(jnp.where(gl == gmax, lane, big), axis=-1, keepdims=True)
    e0 = MOE_GROUPS + EXPERTS_PER_GROUP * gidx
    el = jnp.where((lane >= e0) & (lane < e0 + EXPERTS_PER_GROUP), lg, ninf)
    v1 = jnp.max(el, axis=-1, keepdims=True)
    i1 = jnp.min(jnp.where(el == v1, lane, big), axis=-1, keepdims=True)
    el2 = jnp.where(lane == i1, ninf, el)
    v2 = jnp.max(el2, axis=-1, keepdims=True)
    i2 = jnp.min(jnp.where(el2 == v2, lane, big), axis=-1, keepdims=True)
    r = jnp.exp(v2 - v1)
    wt1 = pg_top * (1.0 / (1.0 + r))
    wt2 = pg_top * (r / (1.0 + r))
    first_lo = i1 < i2
    lo = jnp.where(first_lo, i1, i2) - e0
    hi = jnp.where(first_lo, i2, i1) - e0
    w_lo = jnp.where(first_lo, wt1, wt2)
    w_hi = jnp.where(first_lo, wt2, wt1)
    cls = gidx * N_PAIRS + lo * (7.0 - lo) * 0.5 + (hi - lo - 1.0)
    info = jnp.where(lane == cls, 1.0, 0.0)
    info = jnp.where(lane == LANE_W_LO, w_lo, info)
    info = jnp.where(lane == LANE_W_HI, w_hi, info)
    h2_ref[0, :, d:] = info


def _mix(x, modv, g1, g2, wz, wg, y, ssd_g, yp, ygm, wbs, wbp, wbg, wo, wr, br):
    b, l, d = x.shape
    tm = min(MATMUL_TILE, l)
    tok = lambda width: pl.BlockSpec((1, tm, width), lambda i, j: (i, j, 0))
    full = lambda a: _resident(a.shape)
    return pl.pallas_call(
        _mix_kernel,
        grid=(b, l // tm),
        in_specs=[
            tok(d), pl.BlockSpec((1, SUBLANES, d), lambda i, j: (i, 0, 0)),
            full(g1), full(g2), full(wz), full(wg), tok(y.shape[2]), full(ssd_g),
            tok(yp.shape[2]), tok(ygm.shape[2]), full(wbs), full(wbp), full(wbg), full(wo),
            full(wr), full(br),
        ],
        out_specs=[tok(d), tok(d + LANES)],
        out_shape=[jax.ShapeDtypeStruct((b, l, d), F32),
                   jax.ShapeDtypeStruct((b, l, d + LANES), F32)],
        compiler_params=_params("parallel", "parallel"),
        name="mix",
    )(x, modv, g1, g2, wz, wg, y, ssd_g, yp, ygm, wbs, wbp, wbg, wo, wr, br)


def _sort_kernel(info_ref, pos_ref, tcls_ref, cnt, base, *, tile, n_slabs, n_tile_lanes):
    phase = pl.program_id(0)
    i = pl.program_id(1)
    ts = info_ref.shape[0]
    sub = lax.broadcasted_iota(jnp.int32, (LANES, LANES), 0)
    lan = lax.broadcasted_iota(jnp.int32, (LANES, LANES), 1)

    oh_t = jnp.concatenate(
        [info_ref[k * LANES:(k + 1) * LANES, :].T for k in range(n_slabs)], axis=1)
    csub = lax.broadcasted_iota(jnp.int32, oh_t.shape, 0)
    oh_t = jnp.where(csub < N_CLASSES, oh_t, 0.0)
    tile_count = jnp.sum(oh_t, axis=-1, keepdims=True)

    @pl.when((phase == 0) & (i == 0))
    def _():
        cnt[...] = jnp.zeros_like(cnt)

    @pl.when(phase == 0)
    def _():
        cnt[...] += jnp.broadcast_to(tile_count, cnt.shape)

    @pl.when((phase == 1) & (i == 0))
    def _():
        padded = jnp.ceil(cnt[...] * (1.0 / tile)) * tile
        starts = jnp.dot((lan < sub).astype(F32), padded, precision=HIGHEST,
                         preferred_element_type=F32)
        base[...] = starts
        tcls_ref[...] = jnp.zeros_like(tcls_ref)

    @pl.when(phase == 1)
    def _():
        ti = lax.broadcasted_iota(jnp.int32, (ts, ts), 0)
        tj = lax.broadcasted_iota(jnp.int32, (ts, ts), 1)
        before = jnp.dot(oh_t.astype(BF16), (ti < tj).astype(BF16), preferred_element_type=F32)
        posv = jnp.sum(oh_t * (before + base[:, 0:1]), axis=0, keepdims=True)
        pos_ref[0] = posv.astype(jnp.int32)
        padded = jnp.ceil(cnt[...] * (1.0 / tile)) * tile
        starts = jnp.dot((lan < sub).astype(F32), padded, precision=HIGHEST,
                         preferred_element_type=F32)
        ends = (starts + padded)[:, 0:1]
        tl = lax.broadcasted_iota(jnp.int32, (LANES, n_tile_lanes), 1).astype(F32) * tile
        cs = lax.broadcasted_iota(jnp.int32, (LANES, n_tile_lanes), 0)
        done = jnp.where((cs < N_CLASSES) & (ends <= tl), 1.0, 0.0)
        tc = jnp.minimum(jnp.sum(done, axis=0, keepdims=True), N_CLASSES - 1.0)
        total = jnp.max(jnp.where(cs < N_CLASSES, ends, 0.0), axis=0, keepdims=True)
        valid = jnp.where(tl[0:1, :] < total, 1.0, 0.0)
        tcls_ref[0, 0:1, :] = tc.astype(jnp.int32)
        tcls_ref[0, 1:2, :] = valid.astype(jnp.int32)
        last_tile = jnp.where((sub < N_CLASSES) & (padded > 0.0), starts + padded - tile, -1.0)
        in_use = jnp.max(jnp.where(sub < N_CLASSES, starts + padded, 0.0), axis=0, keepdims=True)
        last_tile = jnp.where(sub == N_CLASSES, in_use, last_tile)
        tcls_ref[0, 2:3, 0:LANES] = last_tile.T[0:1, :].astype(jnp.int32)
        base[...] += jnp.broadcast_to(tile_count, base.shape)


def _route_sort(h2ext, *, d, n_exp_tiles):
    t = h2ext.shape[0]
    ts = min(1024, t)
    nt = t // ts
    n_tile_lanes = -(-n_exp_tiles // LANES) * LANES
    kern = functools.partial(_sort_kernel, tile=EXPERT_TILE, n_slabs=ts // LANES,
                             n_tile_lanes=n_tile_lanes)
    pos, tcls = pl.pallas_call(
        kern,
        grid=(2, nt),
        in_specs=[pl.BlockSpec((ts, LANES), lambda p, i: (i, d // LANES))],
        out_specs=[pl.BlockSpec((1, 1, ts), lambda p, i: (i * p, 0, 0)),
                   pl.BlockSpec((1, SUBLANES, n_tile_lanes), lambda p, i: (0, 0, 0))],
        out_shape=[jax.ShapeDtypeStruct((nt, 1, ts), jnp.int32),
                   jax.ShapeDtypeStruct((1, SUBLANES, n_tile_lanes), jnp.int32)],
        scratch_shapes=[pltpu.VMEM((LANES, LANES), F32), pltpu.VMEM((LANES, LANES), F32)],
        compiler_params=_params("arbitrary", "arbitrary"),
        name="route_sort",
    )(h2ext)
    return (pos.reshape(t), tcls[0, 0, :n_exp_tiles], tcls[0, 1, :n_exp_tiles],
            tcls[0, 2, :N_CLASSES + 1])


def _row_copies(n, make_copy):
    def issue(g, carry):
        for j in range(ROW_UNROLL):
            make_copy(g * ROW_UNROLL + j).start(priority=j % 2)
        return carry

    lax.fori_loop(0, n // ROW_UNROLL, issue, 0)


def _dispatch_kernel(pad_ref, pos_ref, rows_ref, out_ref, zeros, sem, zsem):
    n = rows_ref.shape[0]

    @pl.when(pl.program_id(0) == 0)
    def _():
        zeros[...] = jnp.zeros_like(zeros)
        n_rows = out_ref.shape[0]
        starts = [pad_ref[c] for c in range(N_CLASSES)]
        starts += [pad_ref[N_CLASSES] + k * EXPERT_TILE for k in range(N_CLASSES)]
        wanted = [s >= 0 for s in starts[:N_CLASSES]] + [s < n_rows for s in starts[N_CLASSES:]]

        def fill(s):
            s = pl.multiple_of(jnp.clip(s, 0, n_rows - EXPERT_TILE), EXPERT_TILE)
            return pltpu.make_async_copy(zeros, out_ref.at[pl.ds(s, EXPERT_TILE)], zsem)

        for s, w in zip(starts, wanted):
            pl.when(w)(lambda s=s: fill(s).start())
        for s, w in zip(starts, wanted):
            pl.when(w)(lambda s=s: fill(s).wait())

    _row_copies(n, lambda r: pltpu.make_async_copy(
        rows_ref.at[pl.ds(r, 1)], out_ref.at[pl.ds(pos_ref[0, 0, r], 1)], sem))
    pltpu.make_async_copy(rows_ref, out_ref.at[pl.ds(0, n)], sem).wait()


def _dispatch(h2ext, pos, pad_rows, *, n_sorted_rows):
    t, w = h2ext.shape
    tm = min(ROW_TILE, t)
    nt = t // tm
    grid_spec = pltpu.PrefetchScalarGridSpec(
        num_scalar_prefetch=1,
        grid=(nt,),
        in_specs=[
            pl.BlockSpec((1, 1, tm), lambda i, pad: (i, 0, 0), memory_space=pltpu.SMEM),
            pl.BlockSpec((tm, w), lambda i, pad: (i, 0)),
        ],
        out_specs=pl.BlockSpec(memory_space=pl.ANY),
        scratch_shapes=[pltpu.VMEM((EXPERT_TILE, w), F32), pltpu.SemaphoreType.DMA,
                        pltpu.SemaphoreType.DMA],
    )
    return pl.pallas_call(
        _dispatch_kernel,
        grid_spec=grid_spec,
        out_shape=jax.ShapeDtypeStruct((n_sorted_rows, w), F32),
        compiler_params=_params("arbitrary"),
        name="dispatch",
    )(pad_rows, pos.reshape(nt, 1, tm), h2ext)


def _expert_kernel(elo_ref, ehi_ref, valid_ref, rows_ref, wil_ref, wih_ref, wol_ref, woh_ref,
                   o_ref, *, d, d_expert):
    del elo_ref, ehi_ref
    i = pl.program_id(0)

    @pl.when(valid_ref[i] == 0)
    def _():
        o_ref[...] = jnp.zeros_like(o_ref)

    @pl.when(valid_ref[i] != 0)
    def _():
        h = rows_ref[:, :d].astype(BF16)
        info = rows_ref[:, d:]

        def expert(wi_ref, wo_ref):
            gu = jnp.dot(h, wi_ref[0], preferred_element_type=F32)
            hid = _silu(gu[:, :d_expert]) * gu[:, d_expert:]
            return jnp.dot(hid.astype(BF16), wo_ref[0], preferred_element_type=F32)

        o_ref[...] = (info[:, LANE_W_LO:LANE_W_LO + 1] * expert(wil_ref, wol_ref)
                      + info[:, LANE_W_HI:LANE_W_HI + 1] * expert(wih_ref, woh_ref))


def _experts(sorted_rows, e_lo, e_hi, valid, w_e_in, w_e_out, *, d):
    n_rows, w = sorted_rows.shape
    tm = EXPERT_TILE
    n_tiles = n_rows // tm
    d_expert = w_e_out.shape[1]
    kern = functools.partial(_expert_kernel, d=d, d_expert=d_expert)
    grid_spec = pltpu.PrefetchScalarGridSpec(
        num_scalar_prefetch=3,
        grid=(n_tiles,),
        in_specs=[
            pl.BlockSpec((tm, w), lambda i, lo, hi, v: (i * v[i], 0)),
            pl.BlockSpec((1,) + w_e_in.shape[1:], lambda i, lo, hi, v: (lo[i], 0, 0)),
            pl.BlockSpec((1,) + w_e_in.shape[1:], lambda i, lo, hi, v: (hi[i], 0, 0)),
            pl.BlockSpec((1,) + w_e_out.shape[1:], lambda i, lo, hi, v: (lo[i], 0, 0)),
            pl.BlockSpec((1,) + w_e_out.shape[1:], lambda i, lo, hi, v: (hi[i], 0, 0)),
        ],
        out_specs=pl.BlockSpec((tm, d), lambda i, lo, hi, v: (i, 0)),
    )
    return pl.pallas_call(
        kern,
        grid_spec=grid_spec,
        out_shape=jax.ShapeDtypeStruct((n_rows, d), F32),
        compiler_params=_params("arbitrary"),
        name="experts",
    )(e_lo, e_hi, valid, sorted_rows, w_e_in, w_e_in, w_e_out, w_e_out)


def _combine_kernel(pos_ref, x1_ref, mod_ref, gf_ref, moe_ref, o_ref, rows, sem, *, final):
    n = rows.shape[0]
    _row_copies(n, lambda r: pltpu.make_async_copy(
        moe_ref.at[pl.ds(pos_ref[0, 0, r], 1)], rows.at[pl.ds(r, 1)], sem))
    pltpu.make_async_copy(moe_ref.at[pl.ds(0, n)], rows, sem).wait()
    x2 = x1_ref[...] + mod_ref[0, 5:6, :] * rows[...]
    if final:
        x2 = x2 * lax.rsqrt(jnp.mean(x2 * x2, axis=-1, keepdims=True) + RMS_EPS) * gf_ref[...]
    o_ref[...] = x2


def _combine(x1, pos, modv, g_final, moe_sorted, *, seq_len, final):
    t, d = x1.shape
    tm = min(ROW_TILE, seq_len)
    nt = t // tm
    tiles_per_seq = seq_len // tm
    kern = functools.partial(_combine_kernel, final=final)
    return pl.pallas_call(
        kern,
        grid=(nt,),
        in_specs=[
            pl.BlockSpec((1, 1, tm), lambda i: (i, 0, 0), memory_space=pltpu.SMEM),
            pl.BlockSpec((tm, d), lambda i: (i, 0)),
            pl.BlockSpec((1, SUBLANES, d), lambda i: (i // tiles_per_seq, 0, 0)),
            pl.BlockSpec((1, d), lambda i: (0, 0)),
            pl.BlockSpec(memory_space=pl.ANY),
        ],
        out_specs=pl.BlockSpec((tm, d), lambda i: (i, 0)),
        out_shape=jax.ShapeDtypeStruct((t, d), F32),
        scratch_shapes=[pltpu.VMEM((tm, d), F32), pltpu.SemaphoreType.DMA],
        compiler_params=_params("arbitrary"),
        name="combine",
    )(pos.reshape(nt, 1, tm), x1, modv, g_final, moe_sorted)


def _pair_tables():
    lo, hi = [], []
    for g in range(MOE_GROUPS):
        for a in range(EXPERTS_PER_GROUP):
            for b in range(a + 1, EXPERTS_PER_GROUP):
                lo.append(g * EXPERTS_PER_GROUP + a)
                hi.append(g * EXPERTS_PER_GROUP + b)
    return jnp.asarray(lo, jnp.int32), jnp.asarray(hi, jnp.int32)


def _stream_layer(x, modv, lw, h0f, h0b, *, width, final, g_final, need_tokens=True):
    b, l, d = x.shape
    inner = lw["inner"]
    xab, dtr, pp, ygm = _inproj(x, modv, lw["g1"], lw["wx"], lw["wpg"], lw["wdt"], lw["conv_w8"],
                                lw["conv_b"], lw["ln_g"], lw["ln_b"], lw["ws"], lw["bs_b"],
                                pool_w=lw["pool_cols"], gmlp_w=lw["gmlp_w"])
    yf, hf = _ssd_scan(xab, dtr, lw["dtb_row"], lw["a_row"], h0f, lw["dsk"], reverse=False,
                       inner=inner)
    y, hb = _ssd_scan(xab, dtr, lw["dtb_row"], lw["a_row"], h0b, yf, reverse=True, inner=inner)
    if not need_tokens:
        return None, hf, hb
    yp = _pool_branch(pp, lw["pool_w"], lw["pool_scale"], width=width)
    x1, h2ext = _mix(x, modv, lw["g1"], lw["g2"], lw["wz"], lw["wg"], y, lw["ssd_g"], yp, ygm,
                     lw["wbs"], lw["wbp"], lw["wbg"], lw["wo"], lw["wr"], lw["br"])
    t = b * l
    h2ext = h2ext.reshape(t, d + LANES)
    n_exp_tiles = t // EXPERT_TILE + N_CLASSES
    pos, tcls, valid, pad_rows = _route_sort(h2ext, d=d, n_exp_tiles=n_exp_tiles)
    pair_lo, pair_hi = _pair_tables()
    sorted_rows = _dispatch(h2ext, pos, pad_rows, n_sorted_rows=n_exp_tiles * EXPERT_TILE)
    moe_sorted = _experts(sorted_rows, pair_lo[tcls], pair_hi[tcls], valid, lw["w_e_in"],
                          lw["w_e_out"], d=d)
    x2 = _combine(x1.reshape(t, d), pos, modv, g_final, moe_sorted, seq_len=l, final=final)
    return x2.reshape(b, l, d), hf, hb


def kernel(x, c, ctx, c_ctx, w_mod, b_mod, g_norm1, g_norm2, w_in, conv_w, conv_b, dt_bias, a_log,
           d_skip, ssd_norm_g, pool_w, pool_scale, gmlp_ln_g, gmlp_ln_b, gmlp_ws, gmlp_bs, w_br_ssd,
           w_br_pool, w_br_gmlp, w_o, w_rg, b_rg, w_re, b_re, w_e_in, w_e_out, g_final):
    b, seq, d = x.shape
    depth = w_mod.shape[0]
    n_heads = d_skip.shape[1]
    inner = n_heads * HEAD_DIM
    bcw = SSD_GROUPS * SSD_STATE
    xbc_w = inner + 2 * bcw
    pool_cols = pool_w.shape[1] * pool_w.shape[2]
    gmlp_w = gmlp_ln_g.shape[1]
    off_xbc = inner
    off_dt = off_xbc + xbc_w
    off_pool = off_dt + 2 * n_heads
    off_gmlp = off_pool + pool_cols
    off_gate = off_gmlp + 2 * gmlp_w

    rows = jnp.zeros((SUBLANES, d), F32).at[:b].set(c).at[b].set(c_ctx)
    mod = _modulation(rows, w_mod, b_mod)
    gp = inner // SSD_GROUPS
    zero_state = jnp.zeros((b, SSD_GROUPS, SSD_STATE, gp), F32)
    g_final_row = g_final.reshape(1, d)

    def pad_rows(m):
        return jnp.pad(m.reshape(m.shape[0], 6, d), ((0, 0), (0, 2), (0, 0)))

    for i in range(depth):
        last = i == depth - 1
        mod_lat = pad_rows(mod[i, :b])
        mod_ctx = pad_rows(jnp.broadcast_to(mod[i, b:b + 1], (b, 6 * d)))
        wi = w_in[i]
        a = -jnp.exp(a_log[i].astype(F32)).reshape(1, 2 * n_heads)
        lane_pad = ((0, 0), (0, LANES - 2 * n_heads))
        wr = jnp.pad(jnp.concatenate([w_rg[i], w_re[i]], axis=1),
                     ((0, 0), (0, LANES - MOE_GROUPS - MOE_GROUPS * EXPERTS_PER_GROUP)))
        br = jnp.pad(jnp.concatenate([b_rg[i], b_re[i]]).reshape(1, -1),
                     ((0, 0), (0, LANES - MOE_GROUPS - MOE_GROUPS * EXPERTS_PER_GROUP)))
        lw = dict(
            inner=inner, xbc_w=xbc_w, pool_cols=pool_cols, gmlp_w=gmlp_w,
            g1=g_norm1[i].reshape(1, d), g2=g_norm2[i].reshape(1, d),
            wx=wi[:, off_xbc:off_dt].astype(BF16), wpg=wi[:, off_pool:off_gate].astype(BF16),
            wdt=jnp.pad(wi[:, off_dt:off_pool], lane_pad).astype(BF16),
            wz=wi[:, :inner].astype(BF16), wg=wi[:, off_gate:].astype(BF16),
            ln_g=gmlp_ln_g[i].reshape(1, gmlp_w), ln_b=gmlp_ln_b[i].reshape(1, gmlp_w),
            ws=gmlp_ws[i].astype(BF16),
            bs_b=jnp.broadcast_to(gmlp_bs[i][:, :, None], gmlp_bs[i].shape + (LANES,)),
            conv_w8=jnp.pad(conv_w[i], ((0, SUBLANES - D_CONV), (0, 0))),
            conv_b=conv_b[i].reshape(1, xbc_w),
            dtb_row=jnp.pad(dt_bias[i].reshape(1, 2 * n_heads), lane_pad),
            a_row=jnp.pad(a, lane_pad),
            dsk=jnp.repeat(d_skip[i], HEAD_DIM).reshape(1, inner),
            ssd_g=ssd_norm_g[i].reshape(1, inner),
            pool_w=pool_w[i].astype(BF16), pool_scale=pool_scale[i],
            wbs=w_br_ssd[i].astype(BF16), wbp=w_br_pool[i].astype(BF16),
            wbg=w_br_gmlp[i].astype(BF16), wo=w_o[i].astype(BF16),
            wr=jnp.concatenate([wr.astype(BF16),
                                (wr - wr.astype(BF16).astype(F32)).astype(BF16)], axis=1),
            br=br,
            w_e_in=w_e_in[i].astype(BF16), w_e_out=w_e_out[i].astype(BF16),
        )
        ctx_new, hf_c, hb_c = _stream_layer(ctx, mod_ctx, lw, zero_state, zero_state,
                                            width=ctx.shape[1], final=False, g_final=g_final_row,
                                            need_tokens=not last)
        x, _, _ = _stream_layer(x, mod_lat, lw, hf_c, hb_c, width=GRID_W, final=last,
                                g_final=g_final_row)
        if not last:
            ctx = ctx_new
    return x
```

```python
import functools

import numpy as np
import jax
import jax.numpy as jnp
from jax import lax
from jax.experimental import pallas as pl
from jax.experimental.pallas import tpu as pltpu

F32 = jnp.float32
BF16 = jnp.bfloat16
HIGHEST = lax.Precision.HIGHEST

LANES = 128
SUBLANES = 8
VMEM_LIMIT_BYTES = 56 * 1024 * 1024

HEAD_DIM = 64
SSD_GROUPS = 4
SSD_STATE = 128
CHUNK = 128
D_CONV = 5
POOL_WINDOWS = (2, 4, 8, 16)
GMLP_GROUPS = 4
MOE_GROUPS = 4
EXPERTS_PER_GROUP = 4
N_PAIRS = 6
N_CLASSES = MOE_GROUPS * N_PAIRS
RMS_EPS = 1e-6
LN_EPS = 1e-5
SSD_NORM_EPS = 1e-5
GRID_W = 64

LANE_W_LO = 32
LANE_W_HI = 33

EXPERT_TILE = 256
ROW_TILE = 1024
ROW_UNROLL = 8
MATMUL_TILE = 512
SSD_CHUNKS_PER_STEP = 8


def _params(*sem):
    return pltpu.CompilerParams(dimension_semantics=sem, vmem_limit_bytes=VMEM_LIMIT_BYTES)


def _sigmoid(v):
    return 0.5 * (1.0 + jnp.tanh(0.5 * v))


def _silu(v):
    return v * _sigmoid(v)


def _resident(shape):
    return pl.BlockSpec(shape, lambda *_: (0,) * len(shape), pipeline_mode=pl.Buffered(1))


def _modnorm(x, g, shift, scale):
    y = x * lax.rsqrt(jnp.mean(x * x, axis=-1, keepdims=True) + RMS_EPS)
    return y * g * (1.0 + scale) + shift


def _mod_kernel(c_ref, w_ref, b_ref, o_ref):
    s = _silu(c_ref[...])
    o_ref[0] = jnp.dot(s, w_ref[0], precision=HIGHEST, preferred_element_type=F32) + b_ref[0]


def _modulation(rows, w_mod, b_mod):
    depth, d, n = w_mod.shape
    tn = 1536
    return pl.pallas_call(
        _mod_kernel,
        grid=(depth, n // tn),
        in_specs=[
            pl.BlockSpec((SUBLANES, d), lambda i, j: (0, 0)),
            pl.BlockSpec((1, d, tn), lambda i, j: (i, 0, j)),
            pl.BlockSpec((1, 1, tn), lambda i, j: (i, 0, j)),
        ],
        out_specs=pl.BlockSpec((1, SUBLANES, tn), lambda i, j: (i, 0, j)),
        out_shape=jax.ShapeDtypeStruct((depth, SUBLANES, n), F32),
        compiler_params=_params("parallel", "parallel"),
        name="modulation",
    )(rows, w_mod, b_mod.reshape(depth, 1, n))


def _inproj_kernel(x_ref, xp_ref, xn_ref, mod_ref, g_ref, wx_ref, wpg_ref, wdt_ref, cw_ref, cb_ref,
                   lng_ref, lnb_ref, ws_ref, bs_ref, xab_ref, dt_ref, pp_ref, ygm_ref, xw,
                   *, pool_w, gmlp_w):
    j = pl.program_id(1)
    tm = x_ref.shape[1]

    def normed(v):
        return _modnorm(v, g_ref[...], mod_ref[0, 0:1, :], mod_ref[0, 1:2, :]).astype(BF16)

    h = normed(x_ref[0])
    pv = jnp.where(j > 0, 1.0, 0.0)
    nv = jnp.where(j < pl.num_programs(1) - 1, 1.0, 0.0)
    xc_prev = jnp.dot(normed(xp_ref[0]), wx_ref[...], preferred_element_type=F32) * pv
    xc_cur = jnp.dot(h, wx_ref[...], preferred_element_type=F32)
    xc_next = jnp.dot(normed(xn_ref[0]), wx_ref[...], preferred_element_type=F32) * nv
    for s in range(xw.shape[0]):
        cols = slice(s * LANES, (s + 1) * LANES)
        xw[s, 0:SUBLANES, :] = xc_prev[:, cols]
        xw[s, SUBLANES:SUBLANES + tm, :] = xc_cur[:, cols]
        xw[s, SUBLANES + tm:, :] = xc_next[:, cols]
    for s in range(xw.shape[0]):
        cols = slice(s * LANES, (s + 1) * LANES)
        for rb in range(tm // CHUNK):
            r0 = SUBLANES + rb * CHUNK - D_CONV // 2
            acc = cb_ref[:, cols] + cw_ref[0:1, cols] * xw[s, r0:r0 + CHUNK, :]
            for k in range(1, D_CONV):
                acc = acc + cw_ref[k:k + 1, cols] * xw[s, r0 + k:r0 + k + CHUNK, :]
            xab_ref[0, rb * CHUNK:(rb + 1) * CHUNK, cols] = _silu(acc).astype(BF16)

    dt_ref[0] = jnp.dot(h, wdt_ref[...], preferred_element_type=F32)
    pg = jnp.dot(h, wpg_ref[...], preferred_element_type=F32)
    pp_ref[0] = pg[:, :pool_w]
    uv = pg[:, pool_w:]
    act = uv * (0.5 * (1.0 + jnp.tanh(0.7978845608028654 * (uv + 0.044715 * (uv * uv * uv)))))
    u = act[:, :gmlp_w]
    v = act[:, gmlp_w:]
    mu = jnp.mean(v, axis=-1, keepdims=True)
    vc = v - mu
    var = jnp.mean(vc * vc, axis=-1, keepdims=True)
    vn = (vc * lax.rsqrt(var + LN_EPS) * lng_ref[...] + lnb_ref[...]).astype(BF16)
    gw = gmlp_w // GMLP_GROUPS
    for c in range(tm // CHUNK):
        rows = slice(c * CHUNK, (c + 1) * CHUNK)
        for g in range(GMLP_GROUPS):
            cols = slice(g * gw, (g + 1) * gw)
            mixed = jnp.dot(ws_ref[g], vn[rows, cols], preferred_element_type=F32) + bs_ref[g]
            ygm_ref[0, rows, cols] = (u[rows, cols] * mixed).astype(BF16)


def _inproj(x, modv, g_norm, wx, wpg, wdt, conv_w8, conv_b, ln_g, ln_b, ws, bs_b, *, pool_w, gmlp_w):
    b, l, d = x.shape
    tm = min(MATMUL_TILE, l)
    xbc_w = wx.shape[1]
    hb = tm // SUBLANES
    nhb = l // SUBLANES
    kern = functools.partial(_inproj_kernel, pool_w=pool_w, gmlp_w=gmlp_w)
    tok = lambda width: pl.BlockSpec((1, tm, width), lambda i, j: (i, j, 0))
    full = lambda a: _resident(a.shape)
    return pl.pallas_call(
        kern,
        grid=(b, l // tm),
        in_specs=[
            tok(d),
            pl.BlockSpec((1, SUBLANES, d), lambda i, j: (i, jnp.maximum(j * hb - 1, 0), 0)),
            pl.BlockSpec((1, SUBLANES, d), lambda i, j: (i, jnp.minimum(j * hb + hb, nhb - 1), 0)),
            pl.BlockSpec((1, SUBLANES, d), lambda i, j: (i, 0, 0)),
            full(g_norm), full(wx), full(wpg), full(wdt), full(conv_w8), full(conv_b),
            full(ln_g), full(ln_b), full(ws), full(bs_b),
        ],
        out_specs=[tok(xbc_w), tok(LANES), tok(pool_w), tok(gmlp_w)],
        out_shape=[
            jax.ShapeDtypeStruct((b, l, xbc_w), BF16),
            jax.ShapeDtypeStruct((b, l, LANES), F32),
            jax.ShapeDtypeStruct((b, l, pool_w), F32),
            jax.ShapeDtypeStruct((b, l, gmlp_w), BF16),
        ],
        scratch_shapes=[pltpu.VMEM((xbc_w // LANES, tm + 2 * SUBLANES, LANES), F32)],
        compiler_params=_params("parallel", "parallel"),
        name="inproj",
    )(x, x, x, modv, g_norm, wx, wpg, wdt, conv_w8, conv_b, ln_g, ln_b, ws, bs_b)


def _ssd_kernel(xab_ref, dt_ref, dtb_ref, a_ref, exp_ref, h0_ref, add_ref, y_ref, hout_ref, st,
                *, reverse, n_steps, chunks, inner):
    step = pl.program_id(1)
    bcw = SSD_GROUPS * SSD_STATE
    heads_per_group = inner // HEAD_DIM // SSD_GROUPS
    gp = heads_per_group * HEAD_DIM
    lane0 = (inner // HEAD_DIM) if reverse else 0

    @pl.when(step == 0)
    def _():
        st[...] = h0_ref[0]

    ii = lax.broadcasted_iota(jnp.int32, (CHUNK, CHUNK), 0)
    jj = lax.broadcasted_iota(jnp.int32, (CHUNK, CHUNK), 1)
    causal = (ii <= jj) if reverse else (ii >= jj)
    tri = causal.astype(BF16)
    last_row = 0 if reverse else CHUNK - 1
    head_of_lane = lax.broadcasted_iota(jnp.int32, (CHUNK, gp), 1) // HEAD_DIM

    def split(v, terms):
        out = []
        for _ in range(terms - 1):
            out.append(v.astype(BF16))
            v = v - out[-1].astype(F32)
        return out + [v.astype(BF16)]

    order = range(chunks - 1, -1, -1) if reverse else range(chunks)
    dtms, parts = {}, []
    for ci in order:
        z = dt_ref[0, ci * CHUNK:(ci + 1) * CHUNK, :] + dtb_ref[...]
        dtms[ci] = jnp.maximum(z, 0.0) + jnp.log(1.0 + jnp.exp(-jnp.abs(z)))
        parts += split(dtms[ci] * a_ref[...], 3)
    cums = jnp.dot(tri, jnp.concatenate(parts, axis=1), preferred_element_type=F32)
    acums, lhs = {}, []
    for n, ci in enumerate(order):
        c3 = cums[:, 3 * n * LANES:3 * (n + 1) * LANES]
        acum = c3[:, :LANES] + c3[:, LANES:2 * LANES] + c3[:, 2 * LANES:]
        acums[ci] = acum
        alast = acum[last_row:last_row + 1, :]
        lhs.append(jnp.concatenate(split(jnp.exp(acum), 2), axis=1))
        lhs.append(jnp.concatenate(split(dtms[ci] * jnp.exp(alast - acum), 2), axis=1))
    per_channel = jnp.dot(jnp.concatenate(lhs, axis=0), exp_ref[...], preferred_element_type=F32)
    coeffs = {}
    for n, ci in enumerate(order):
        r0 = 2 * n * CHUNK
        coeffs[ci] = (acums[ci], acums[ci].T, dtms[ci].T, per_channel[r0:r0 + CHUNK],
                      per_channel[r0 + CHUNK:r0 + 2 * CHUNK])

    for ci in order:
        rows = slice(ci * CHUNK, (ci + 1) * CHUNK)
        acum, acum_t, dt_t, eac_c, wst_c = coeffs[ci]

        def cols_f32(lo, hi):
            return xab_ref[0, rows, lo:hi].astype(F32)

        def cols_bf16(lo, hi):
            return xab_ref[0, rows, lo:hi]

        for g in range(SSD_GROUPS):
            b0 = inner + g * SSD_STATE
            c0 = inner + bcw + g * SSD_STATE
            gcols = slice(g * gp, (g + 1) * gp)
            cg = cols_bf16(c0, c0 + SSD_STATE)
            bgt = cols_f32(b0, b0 + SSD_STATE).T.astype(BF16)
            cbm = jnp.dot(cg, bgt, preferred_element_type=F32)
            sg = st[g]
            yoff = jnp.dot(cg, sg.astype(BF16), preferred_element_type=F32)
            xs_f = cols_f32(g * gp, (g + 1) * gp)
            xs_b = cols_bf16(g * gp, (g + 1) * gp)
            ms, xbd = [], []
            for k in range(heads_per_group):
                ln = lane0 + g * heads_per_group + k
                seg = acum[:, ln:ln + 1] - acum_t[ln:ln + 1, :]
                dec = jnp.exp(jnp.where(causal, seg, -jnp.inf))
                ms.append((cbm * dec * dt_t[ln:ln + 1, :]).astype(BF16))
                xbd.append(jnp.where(head_of_lane == k, xs_b, jnp.zeros_like(xs_b)))
            yd = jnp.dot(jnp.concatenate(ms, axis=1), jnp.concatenate(xbd, axis=0),
                         preferred_element_type=F32)
            yg = yd + yoff * eac_c[:, gcols]
            if reverse:
                yg = yg + add_ref[0, rows, gcols]
            else:
                yg = yg + add_ref[:, gcols] * xs_f
            y_ref[0, rows, gcols] = yg
            xd = (xs_f * wst_c[:, gcols]).astype(BF16)
            st[g] = (sg * eac_c[last_row:last_row + 1, gcols]
                     + jnp.dot(bgt, xd, preferred_element_type=F32))

    @pl.when(step == n_steps - 1)
    def _():
        hout_ref[0] = st[...]


def _head_expand(inner, lane0):
    e = np.zeros((LANES, inner), np.float32)
    for h in range(inner // HEAD_DIM):
        e[lane0 + h, h * HEAD_DIM:(h + 1) * HEAD_DIM] = 1.0
    return jnp.asarray(np.concatenate([e, e], axis=0), BF16)


def _ssd_scan(xab, dtr, dtb_row, a_row, h0, add, *, reverse, inner):
    b, l, xw_cols = xab.shape
    chunks = min(SSD_CHUNKS_PER_STEP, l // CHUNK)
    n_steps = l // (chunks * CHUNK)
    gp = inner // SSD_GROUPS
    block_of = (lambda j: n_steps - 1 - j) if reverse else (lambda j: j)
    cur = lambda width: pl.BlockSpec((1, chunks * CHUNK, width), lambda i, j: (i, block_of(j), 0))
    state = pl.BlockSpec((1, SSD_GROUPS, SSD_STATE, gp), lambda i, j: (i, 0, 0, 0))
    lane0 = (inner // HEAD_DIM) if reverse else 0
    kern = functools.partial(_ssd_kernel, reverse=reverse, n_steps=n_steps, chunks=chunks,
                             inner=inner)
    return pl.pallas_call(
        kern,
        grid=(b, n_steps),
        in_specs=[cur(xw_cols), cur(LANES), _resident((1, LANES)), _resident((1, LANES)),
                  _resident((2 * LANES, inner)), state,
                  cur(inner) if reverse else _resident((1, inner))],
        out_specs=[cur(inner), state],
        out_shape=[jax.ShapeDtypeStruct((b, l, inner), F32),
                   jax.ShapeDtypeStruct((b, SSD_GROUPS, SSD_STATE, gp), F32)],
        scratch_shapes=[pltpu.VMEM((SSD_GROUPS, SSD_STATE, gp), F32)],
        compiler_params=_params("parallel", "arbitrary"),
        name="ssd_bwd" if reverse else "ssd_fwd",
    )(xab, dtr, dtb_row, a_row, _head_expand(inner, lane0), h0, add)


def _pool_kernel(pp_ref, band_ref, cinv_ref, inv_ref, pw_ref, ps_ref, o_ref, cp, *, width, rows,
                 slab):
    g = pl.program_id(1)
    pad = SUBLANES * width
    l = rows * width
    rows_per_slab = slab // width

    def body(window):
        lo = window // 2
        hi = window - lo
        cp[0:pad, :] = jnp.zeros((pad, LANES), F32)
        cp[pad + l:, :] = jnp.zeros((pad, LANES), F32)

        def col_pool(s, carry):
            off = pl.multiple_of(s * slab, slab)
            v = pp_ref[0, pl.ds(off, slab), :]
            v_hi = v.astype(BF16)
            v_lo = (v - v_hi.astype(F32)).astype(BF16)
            sums = jnp.dot(band_ref[0], jnp.concatenate([v_hi, v_lo], axis=1),
                           preferred_element_type=F32)
            cp[pl.ds(pad + off, slab), :] = (sums[:, :LANES] + sums[:, LANES:]) * cinv_ref[0]
            return carry

        lax.fori_loop(0, l // slab, col_pool, 0, unroll=min(4, l // slab))

        def row_pool(s, carry):
            off = pl.multiple_of(s * slab, slab)
            parts = []
            for r in range(rows_per_slab):
                base = pad + off + r * width
                acc = cp[pl.ds(base - lo * width, width), :]
                for k in range(1 - lo, hi):
                    acc = acc + cp[pl.ds(base + k * width, width), :]
                parts.append(acc * inv_ref[0, pl.ds(s * rows_per_slab + r, 1), :])
            pooled = parts[0] if len(parts) == 1 else jnp.concatenate(parts, axis=0)
            d = (pooled - pp_ref[0, pl.ds(off, slab), :]).astype(BF16)
            y = jnp.dot(d, pw_ref[0], preferred_element_type=F32) * ps_ref[0]
            o_ref[0, pl.ds(off, slab), :] = y.astype(BF16)
            return carry

        lax.fori_loop(0, l // slab, row_pool, 0, unroll=min(4, l // slab))

    for gi, window in enumerate(POOL_WINDOWS):
        pl.when(g == gi)(functools.partial(body, window))


def _pool_tables(l, width):
    rows = l // width
    slab = max(2 * CHUNK, width)
    n_g = len(POOL_WINDOWS)
    band = np.zeros((n_g, slab, slab), np.float32)
    cinv = np.ones((n_g, slab, LANES), np.float32)
    inv = np.ones((n_g, max(rows, SUBLANES), LANES), np.float32)
    for gi, w in enumerate(POOL_WINDOWS):
        lo, hi = w // 2, w - w // 2
        for t in range(slab):
            r, c = divmod(t, width)
            c0, c1 = max(c - lo, 0), min(c + hi, width)
            band[gi, t, r * width + c0:r * width + c1] = 1.0
            cinv[gi, t, :] = 1.0 / (c1 - c0)
        for r in range(rows):
            inv[gi, r, :] = 1.0 / (min(r + hi, rows) - max(r - lo, 0))
    return jnp.asarray(band, BF16), jnp.asarray(cinv), jnp.asarray(inv), rows, slab


def _pool_branch(pp, pool_w, pool_scale, *, width):
    b, l, pw = pp.shape
    n_g = len(POOL_WINDOWS)
    band, cinv, inv, rows, slab = _pool_tables(l, width)
    kern = functools.partial(_pool_kernel, width=width, rows=rows, slab=slab)
    return pl.pallas_call(
        kern,
        grid=(b, n_g),
        in_specs=[
            pl.BlockSpec((1, l, LANES), lambda i, g: (i, 0, g)),
            pl.BlockSpec((1, slab, slab), lambda i, g: (g, 0, 0)),
            pl.BlockSpec((1, slab, LANES), lambda i, g: (g, 0, 0)),
            pl.BlockSpec((1,) + inv.shape[1:], lambda i, g: (g, 0, 0)),
            pl.BlockSpec((1, LANES, LANES), lambda i, g: (g, 0, 0)),
            pl.BlockSpec((1, 1, LANES), lambda i, g: (g, 0, 0)),
        ],
        out_specs=pl.BlockSpec((1, l, LANES), lambda i, g: (i, 0, g)),
        out_shape=jax.ShapeDtypeStruct((b, l, pw), BF16),
        scratch_shapes=[pltpu.VMEM((l + 2 * SUBLANES * width, LANES), F32)],
        compiler_params=_params("parallel", "parallel"),
        name="pool",
    )(pp, band, cinv, inv, pool_w, pool_scale.reshape(n_g, 1, LANES))


def _mix_kernel(x_ref, mod_ref, g1_ref, g2_ref, wz_ref, wg_ref, y_ref, sg_ref, yp_ref, ygm_ref,
                wbs_ref, wbp_ref, wbg_ref, wo_ref, wr_ref, br_ref, x1_ref, h2_ref):
    x = x_ref[0]
    d = x.shape[1]
    h = _modnorm(x, g1_ref[...], mod_ref[0, 0:1, :], mod_ref[0, 1:2, :]).astype(BF16)
    z = jnp.dot(h, wz_ref[...], preferred_element_type=F32)
    t = y_ref[0] * _silu(z)
    ys = t * lax.rsqrt(jnp.mean(t * t, axis=-1, keepdims=True) + SSD_NORM_EPS) * sg_ref[...]
    gates = _sigmoid(jnp.dot(h, wg_ref[...], preferred_element_type=F32))
    merged = gates[:, :d] * jnp.dot(ys.astype(BF16), wbs_ref[...], preferred_element_type=F32)
    merged = merged + gates[:, d:2 * d] * jnp.dot(yp_ref[0], wbp_ref[...], preferred_element_type=F32)
    merged = merged + gates[:, 2 * d:] * jnp.dot(ygm_ref[0], wbg_ref[...], preferred_element_type=F32)
    mix = jnp.dot(merged.astype(BF16), wo_ref[...], preferred_element_type=F32)
    x1 = x + mod_ref[0, 2:3, :] * mix
    x1_ref[0] = x1
    h2 = _modnorm(x1, g2_ref[...], mod_ref[0, 3:4, :], mod_ref[0, 4:5, :])
    h2_ref[0, :, :d] = h2

    h2_hi = h2.astype(BF16)
    h2_lo = (h2 - h2_hi.astype(F32)).astype(BF16)
    p_hi = jnp.dot(h2_hi, wr_ref[...], preferred_element_type=F32)
    p_lo = jnp.dot(h2_lo, wr_ref[:, :LANES], preferred_element_type=F32)
    lg = p_hi[:, :LANES] + p_hi[:, LANES:] + p_lo + br_ref[...]
    lane = lax.broadcasted_iota(jnp.int32, lg.shape, 1).astype(F32)
    ninf = -jnp.inf
    big = float(LANES)
    gl = jnp.where(lane < MOE_GROUPS, lg, ninf)
    gmax = jnp.max(gl, axis=-1, keepdims=True)
    pg_top = 1.0 / jnp.sum(jnp.exp(gl - gmax), axis=-1, keepdims=True)
    gidx = jnp.min(jnp.where(gl == gmax, lane, big), axis=-1, keepdims=True)
    e0 = MOE_GROUPS + EXPERTS_PER_GROUP * gidx
    el = jnp.where((lane >= e0) & (lane < e0 + EXPERTS_PER_GROUP), lg, ninf)
    v1 = jnp.max(el, axis=-1, keepdims=True)
    i1 = jnp.min(jnp.where(el == v1, lane, big), axis=-1, keepdims=True)
    el2 = jnp.where(lane == i1, ninf, el)
    v2 = jnp.max(el2, axis=-1, keepdims=True)
    i2 = jnp.min(jnp.where(el2 == v2, lane, big), axis=-1, keepdims=True)
    r = jnp.exp(v2 - v1)
    wt1 = pg_top * (1.0 / (1.0 + r))
    wt2 = pg_top * (r / (1.0 + r))
    first_lo = i1 < i2
    lo = jnp.where(first_lo, i1, i2) - e0
    hi = jnp.where(first_lo, i2, i1) - e0
    w_lo = jnp.where(first_lo, wt1, wt2)
    w_hi = jnp.where(first_lo, wt2, wt1)
    cls = gidx * N_PAIRS + lo * (7.0 - lo) * 0.5 + (hi - lo - 1.0)
    info = jnp.where(lane == cls, 1.0, 0.0)
    info = jnp.where(lane == LANE_W_LO, w_lo, info)
    info = jnp.where(lane == LANE_W_HI, w_hi, info)
    h2_ref[0, :, d:] = info


def _mix(x, modv, g1, g2, wz, wg, y, ssd_g, yp, ygm, wbs, wbp, wbg, wo, wr, br):
    b, l, d = x.shape
    tm = min(MATMUL_TILE, l)
    tok = lambda width: pl.BlockSpec((1, tm, width), lambda i, j: (i, j, 0))
    full = lambda a: _resident(a.shape)
    return pl.pallas_call(
        _mix_kernel,
        grid=(b, l // tm),
        in_specs=[
            tok(d), pl.BlockSpec((1, SUBLANES, d), lambda i, j: (i, 0, 0)),
            full(g1), full(g2), full(wz), full(wg), tok(y.shape[2]), full(ssd_g),
            tok(yp.shape[2]), tok(ygm.shape[2]), full(wbs), full(wbp), full(wbg), full(wo),
            full(wr), full(br),
        ],
        out_specs=[tok(d), tok(d + LANES)],
        out_shape=[jax.ShapeDtypeStruct((b, l, d), F32),
                   jax.ShapeDtypeStruct((b, l, d + LANES), F32)],
        compiler_params=_params("parallel", "parallel"),
        name="mix",
    )(x, modv, g1, g2, wz, wg, y, ssd_g, yp, ygm, wbs, wbp, wbg, wo, wr, br)


def _sort_kernel(info_ref, pos_ref, tcls_ref, cnt, base, *, tile, n_slabs, n_tile_lanes):
    phase = pl.program_id(0)
    i = pl.program_id(1)
    ts = info_ref.shape[0]
    sub = lax.broadcasted_iota(jnp.int32, (LANES, LANES), 0)
    lan = lax.broadcasted_iota(jnp.int32, (LANES, LANES), 1)

    @pl.when((phase == 0) & (i == 0))
    def _():
        cnt[...] = jnp.zeros_like(cnt)

    @pl.when(phase == 0)
    def _():
        lane = lax.broadcasted_iota(jnp.int32, (ts, LANES), 1)
        one_hot = jnp.where(lane < N_CLASSES, info_ref[...], 0.0)
        cnt[...] += jnp.broadcast_to(jnp.sum(one_hot, axis=0, keepdims=True), cnt.shape)

    @pl.when((phase == 1) & (i == 0))
    def _():
        counts = cnt[...].T
        padded = jnp.ceil(counts * (1.0 / tile)) * tile
        starts = jnp.dot((lan < sub).astype(F32), padded, precision=HIGHEST,
                         preferred_element_type=F32)
        base[...] = starts
        tcls_ref[...] = jnp.zeros_like(tcls_ref)
        ends = (starts + padded)[:, 0:1]
        tl = lax.broadcasted_iota(jnp.int32, (LANES, n_tile_lanes), 1).astype(F32) * tile
        cs = lax.broadcasted_iota(jnp.int32, (LANES, n_tile_lanes), 0)
        done = jnp.where((cs < N_CLASSES) & (ends <= tl), 1.0, 0.0)
        tc = jnp.minimum(jnp.sum(done, axis=0, keepdims=True), N_CLASSES - 1.0)
        total = jnp.max(jnp.where(cs < N_CLASSES, ends, 0.0), axis=0, keepdims=True)
        valid = jnp.where(tl[0:1, :] < total, 1.0, 0.0)
        tcls_ref[0, 0:1, :] = tc.astype(jnp.int32)
        tcls_ref[0, 1:2, :] = valid.astype(jnp.int32)
        last_tile = jnp.where((sub < N_CLASSES) & (padded > 0.0), starts + padded - tile, -1.0)
        in_use = jnp.max(jnp.where(sub < N_CLASSES, starts + padded, 0.0), axis=0, keepdims=True)
        last_tile = jnp.where(sub == N_CLASSES, in_use, last_tile)
        tcls_ref[0, 2:3, 0:LANES] = last_tile.T[0:1, :].astype(jnp.int32)

    @pl.when(phase == 1)
    def _():
        oh_t = jnp.concatenate(
            [info_ref[k * LANES:(k + 1) * LANES, :].T for k in range(n_slabs)], axis=1)
        csub = lax.broadcasted_iota(jnp.int32, oh_t.shape, 0)
        oh_t = jnp.where(csub < N_CLASSES, oh_t, 0.0)
        ti = lax.broadcasted_iota(jnp.int32, (ts, ts), 0)
        tj = lax.broadcasted_iota(jnp.int32, (ts, ts), 1)
        before = jnp.dot(oh_t.astype(BF16), (ti < tj).astype(BF16), preferred_element_type=F32)
        posv = jnp.sum(oh_t * (before + base[:, 0:1]), axis=0, keepdims=True)
        pos_ref[0] = posv.astype(jnp.int32)
        base[...] += jnp.broadcast_to(jnp.sum(oh_t, axis=-1, keepdims=True), base.shape)


def _route_sort(h2ext, *, d, n_exp_tiles):
    t = h2ext.shape[0]
    ts = min(1024, t)
    nt = t // ts
    n_tile_lanes = -(-n_exp_tiles // LANES) * LANES
    kern = functools.partial(_sort_kernel, tile=EXPERT_TILE, n_slabs=ts // LANES,
                             n_tile_lanes=n_tile_lanes)
    pos, tcls = pl.pallas_call(
        kern,
        grid=(2, nt),
        in_specs=[pl.BlockSpec((ts, LANES), lambda p, i: (i, d // LANES))],
        out_specs=[pl.BlockSpec((1, 1, ts), lambda p, i: (i * p, 0, 0)),
                   pl.BlockSpec((1, SUBLANES, n_tile_lanes), lambda p, i: (0, 0, 0))],
        out_shape=[jax.ShapeDtypeStruct((nt, 1, ts), jnp.int32),
                   jax.ShapeDtypeStruct((1, SUBLANES, n_tile_lanes), jnp.int32)],
        scratch_shapes=[pltpu.VMEM((LANES, LANES), F32), pltpu.VMEM((LANES, LANES), F32)],
        compiler_params=_params("arbitrary", "arbitrary"),
        name="route_sort",
    )(h2ext)
    return (pos.reshape(t), tcls[0, 0, :n_exp_tiles], tcls[0, 1, :n_exp_tiles],
            tcls[0, 2, :N_CLASSES + 1])


def _row_copies(n_groups, make_copy):
    def issue(g, carry):
        for j in range(SUBLANES):
            make_copy(g, j).start(priority=j % 2)
        return carry

    lax.fori_loop(0, n_groups, issue, 0)


def _row_of(ref, r):
    return ref.at[lax.shift_right_logical(r, 3), pl.ds(lax.bitwise_and(r, SUBLANES - 1), 1)]


def _dispatch_kernel(pad_ref, pos_ref, rows_ref, out_ref, zeros, sem, zsem):
    n_groups = rows_ref.shape[0]

    @pl.when(pl.program_id(0) == 0)
    def _():
        zeros[...] = jnp.zeros_like(zeros)
        n_rows = out_ref.shape[0] * SUBLANES
        tile_groups = EXPERT_TILE // SUBLANES
        starts = [pad_ref[c] for c in range(N_CLASSES)]
        starts += [pad_ref[N_CLASSES] + k * EXPERT_TILE for k in range(N_CLASSES)]
        wanted = [s >= 0 for s in starts[:N_CLASSES]] + [s < n_rows for s in starts[N_CLASSES:]]

        def fill(s):
            g0 = pl.multiple_of(
                lax.shift_right_logical(jnp.clip(s, 0, n_rows - EXPERT_TILE), 3), tile_groups)
            return pltpu.make_async_copy(zeros, out_ref.at[pl.ds(g0, tile_groups)], zsem)

        for s, w in zip(starts, wanted):
            pl.when(w)(lambda s=s: fill(s).start())
        for s, w in zip(starts, wanted):
            pl.when(w)(lambda s=s: fill(s).wait())

    _row_copies(n_groups, lambda g, j: pltpu.make_async_copy(
        rows_ref.at[g, pl.ds(j, 1)], _row_of(out_ref, pos_ref[0, 0, g * SUBLANES + j]), sem))
    pltpu.make_async_copy(rows_ref, out_ref.at[pl.ds(0, n_groups)], sem).wait()


def _dispatch(h2ext, pos, pad_rows, *, n_sorted_rows):
    t, w = h2ext.shape
    tm = min(ROW_TILE, t)
    nt = t // tm
    grid_spec = pltpu.PrefetchScalarGridSpec(
        num_scalar_prefetch=1,
        grid=(nt,),
        in_specs=[
            pl.BlockSpec((1, 1, tm), lambda i, pad: (i, 0, 0), memory_space=pltpu.SMEM),
            pl.BlockSpec((tm // SUBLANES, SUBLANES, w), lambda i, pad: (i, 0, 0)),
        ],
        out_specs=pl.BlockSpec(memory_space=pl.ANY),
        scratch_shapes=[pltpu.VMEM((EXPERT_TILE // SUBLANES, SUBLANES, w), F32),
                        pltpu.SemaphoreType.DMA, pltpu.SemaphoreType.DMA],
    )
    out = pl.pallas_call(
        _dispatch_kernel,
        grid_spec=grid_spec,
        out_shape=jax.ShapeDtypeStruct((n_sorted_rows // SUBLANES, SUBLANES, w), F32),
        compiler_params=_params("arbitrary"),
        name="dispatch",
    )(pad_rows, pos.reshape(nt, 1, tm), h2ext.reshape(t // SUBLANES, SUBLANES, w))
    return out.reshape(n_sorted_rows, w)


def _expert_kernel(elo_ref, ehi_ref, valid_ref, rows_ref, wil_ref, wih_ref, wol_ref, woh_ref,
                   o_ref, *, d, d_expert):
    del elo_ref, ehi_ref
    i = pl.program_id(0)

    @pl.when(valid_ref[i] == 0)
    def _():
        o_ref[...] = jnp.zeros_like(o_ref)

    @pl.when(valid_ref[i] != 0)
    def _():
        h = rows_ref[:, :d].astype(BF16)
        info = rows_ref[:, d:]

        def expert(wi_ref, wo_ref):
            gu = jnp.dot(h, wi_ref[0], preferred_element_type=F32)
            hid = _silu(gu[:, :d_expert]) * gu[:, d_expert:]
            return jnp.dot(hid.astype(BF16), wo_ref[0], preferred_element_type=F32)

        o_ref[...] = (info[:, LANE_W_LO:LANE_W_LO + 1] * expert(wil_ref, wol_ref)
                      + info[:, LANE_W_HI:LANE_W_HI + 1] * expert(wih_ref, woh_ref))


def _experts(sorted_rows, e_lo, e_hi, valid, w_e_in, w_e_out, *, d):
    n_rows, w = sorted_rows.shape
    tm = EXPERT_TILE
    n_tiles = n_rows // tm
    d_expert = w_e_out.shape[1]
    kern = functools.partial(_expert_kernel, d=d, d_expert=d_expert)
    grid_spec = pltpu.PrefetchScalarGridSpec(
        num_scalar_prefetch=3,
        grid=(n_tiles,),
        in_specs=[
            pl.BlockSpec((tm, w), lambda i, lo, hi, v: (i * v[i], 0)),
            pl.BlockSpec((1,) + w_e_in.shape[1:], lambda i, lo, hi, v: (lo[i], 0, 0)),
            pl.BlockSpec((1,) + w_e_in.shape[1:], lambda i, lo, hi, v: (hi[i], 0, 0)),
            pl.BlockSpec((1,) + w_e_out.shape[1:], lambda i, lo, hi, v: (lo[i], 0, 0)),
            pl.BlockSpec((1,) + w_e_out.shape[1:], lambda i, lo, hi, v: (hi[i], 0, 0)),
        ],
        out_specs=pl.BlockSpec((tm, d), lambda i, lo, hi, v: (i, 0)),
    )
    return pl.pallas_call(
        kern,
        grid_spec=grid_spec,
        out_shape=jax.ShapeDtypeStruct((n_rows, d), F32),
        compiler_params=_params("arbitrary"),
        name="experts",
    )(e_lo, e_hi, valid, sorted_rows, w_e_in, w_e_in, w_e_out, w_e_out)


def _combine_kernel(pos_ref, x1_ref, mod_ref, gf_ref, moe_ref, o_ref, rows, sem, *, final):
    n_groups = rows.shape[0]
    _row_copies(n_groups, lambda g, j: pltpu.make_async_copy(
        _row_of(moe_ref, pos_ref[0, 0, g * SUBLANES + j]), rows.at[g, pl.ds(j, 1)], sem))
    pltpu.make_async_copy(moe_ref.at[pl.ds(0, n_groups)], rows, sem).wait()
    x2 = x1_ref[...] + mod_ref[0, 5:6, :] * rows[...]
    if final:
        x2 = x2 * lax.rsqrt(jnp.mean(x2 * x2, axis=-1, keepdims=True) + RMS_EPS) * gf_ref[...]
    o_ref[...] = x2


def _combine(x1, pos, modv, g_final, moe_sorted, *, seq_len, final):
    t, d = x1.shape
    tm = min(ROW_TILE, seq_len)
    nt = t // tm
    tiles_per_seq = seq_len // tm
    kern = functools.partial(_combine_kernel, final=final)
    grouped = lambda a: a.reshape(a.shape[0] // SUBLANES, SUBLANES, a.shape[1])
    tile = pl.BlockSpec((tm // SUBLANES, SUBLANES, d), lambda i: (i, 0, 0))
    out = pl.pallas_call(
        kern,
        grid=(nt,),
        in_specs=[
            pl.BlockSpec((1, 1, tm), lambda i: (i, 0, 0), memory_space=pltpu.SMEM),
            tile,
            pl.BlockSpec((1, SUBLANES, d), lambda i: (i // tiles_per_seq, 0, 0)),
            pl.BlockSpec((1, d), lambda i: (0, 0)),
            pl.BlockSpec(memory_space=pl.ANY),
        ],
        out_specs=tile,
        out_shape=jax.ShapeDtypeStruct((t // SUBLANES, SUBLANES, d), F32),
        scratch_shapes=[pltpu.VMEM((tm // SUBLANES, SUBLANES, d), F32), pltpu.SemaphoreType.DMA],
        compiler_params=_params("arbitrary"),
        name="combine",
    )(pos.reshape(nt, 1, tm), grouped(x1), modv, g_final, grouped(moe_sorted))
    return out.reshape(t, d)


def _pair_tables():
    lo, hi = [], []
    for g in range(MOE_GROUPS):
        for a in range(EXPERTS_PER_GROUP):
            for b in range(a + 1, EXPERTS_PER_GROUP):
                lo.append(g * EXPERTS_PER_GROUP + a)
                hi.append(g * EXPERTS_PER_GROUP + b)
    return jnp.asarray(lo, jnp.int32), jnp.asarray(hi, jnp.int32)


def _stream_layer(x, modv, lw, h0f, h0b, *, width, final, g_final, need_tokens=True):
    b, l, d = x.shape
    inner = lw["inner"]
    xab, dtr, pp, ygm = _inproj(x, modv, lw["g1"], lw["wx"], lw["wpg"], lw["wdt"], lw["conv_w8"],
                                lw["conv_b"], lw["ln_g"], lw["ln_b"], lw["ws"], lw["bs_b"],
                                pool_w=lw["pool_cols"], gmlp_w=lw["gmlp_w"])
    yf, hf = _ssd_scan(xab, dtr, lw["dtb_row"], lw["a_row"], h0f, lw["dsk"], reverse=False,
                       inner=inner)
    y, hb = _ssd_scan(xab, dtr, lw["dtb_row"], lw["a_row"], h0b, yf, reverse=True, inner=inner)
    if not need_tokens:
        return None, hf, hb
    yp = _pool_branch(pp, lw["pool_w"], lw["pool_scale"], width=width)
    x1, h2ext = _mix(x, modv, lw["g1"], lw["g2"], lw["wz"], lw["wg"], y, lw["ssd_g"], yp, ygm,
                     lw["wbs"], lw["wbp"], lw["wbg"], lw["wo"], lw["wr"], lw["br"])
    t = b * l
    h2ext = h2ext.reshape(t, d + LANES)
    n_exp_tiles = t // EXPERT_TILE + N_CLASSES
    pos, tcls, valid, pad_rows = _route_sort(h2ext, d=d, n_exp_tiles=n_exp_tiles)
    pair_lo, pair_hi = _pair_tables()
    sorted_rows = _dispatch(h2ext, pos, pad_rows, n_sorted_rows=n_exp_tiles * EXPERT_TILE)
    moe_sorted = _experts(sorted_rows, pair_lo[tcls], pair_hi[tcls], valid, lw["w_e_in"],
                          lw["w_e_out"], d=d)
    x2 = _combine(x1.reshape(t, d), pos, modv, g_final, moe_sorted, seq_len=l, final=final)
    return x2.reshape(b, l, d), hf, hb


def kernel(x, c, ctx, c_ctx, w_mod, b_mod, g_norm1, g_norm2, w_in, conv_w, conv_b, dt_bias, a_log,
           d_skip, ssd_norm_g, pool_w, pool_scale, gmlp_ln_g, gmlp_ln_b, gmlp_ws, gmlp_bs, w_br_ssd,
           w_br_pool, w_br_gmlp, w_o, w_rg, b_rg, w_re, b_re, w_e_in, w_e_out, g_final):
    b, seq, d = x.shape
    depth = w_mod.shape[0]
    n_heads = d_skip.shape[1]
    inner = n_heads * HEAD_DIM
    bcw = SSD_GROUPS * SSD_STATE
    xbc_w = inner + 2 * bcw
    pool_cols = pool_w.shape[1] * pool_w.shape[2]
    gmlp_w = gmlp_ln_g.shape[1]
    off_xbc = inner
    off_dt = off_xbc + xbc_w
    off_pool = off_dt + 2 * n_heads
    off_gmlp = off_pool + pool_cols
    off_gate = off_gmlp + 2 * gmlp_w

    rows = jnp.zeros((SUBLANES, d), F32).at[:b].set(c).at[b].set(c_ctx)
    mod = _modulation(rows, w_mod, b_mod)
    gp = inner // SSD_GROUPS
    zero_state = jnp.zeros((b, SSD_GROUPS, SSD_STATE, gp), F32)
    g_final_row = g_final.reshape(1, d)

    def pad_rows(m):
        return jnp.pad(m.reshape(m.shape[0], 6, d), ((0, 0), (0, 2), (0, 0)))

    for i in range(depth):
        last = i == depth - 1
        mod_lat = pad_rows(mod[i, :b])
        mod_ctx = pad_rows(jnp.broadcast_to(mod[i, b:b + 1], (b, 6 * d)))
        wi = w_in[i]
        a = -jnp.exp(a_log[i].astype(F32)).reshape(1, 2 * n_heads)
        lane_pad = ((0, 0), (0, LANES - 2 * n_heads))
        wr = jnp.pad(jnp.concatenate([w_rg[i], w_re[i]], axis=1),
                     ((0, 0), (0, LANES - MOE_GROUPS - MOE_GROUPS * EXPERTS_PER_GROUP)))
        br = jnp.pad(jnp.concatenate([b_rg[i], b_re[i]]).reshape(1, -1),
                     ((0, 0), (0, LANES - MOE_GROUPS - MOE_GROUPS * EXPERTS_PER_GROUP)))
        lw = dict(
            inner=inner, xbc_w=xbc_w, pool_cols=pool_cols, gmlp_w=gmlp_w,
            g1=g_norm1[i].reshape(1, d), g2=g_norm2[i].reshape(1, d),
            wx=wi[:, off_xbc:off_dt].astype(BF16), wpg=wi[:, off_pool:off_gate].astype(BF16),
            wdt=jnp.pad(wi[:, off_dt:off_pool], lane_pad).astype(BF16),
            wz=wi[:, :inner].astype(BF16), wg=wi[:, off_gate:].astype(BF16),
            ln_g=gmlp_ln_g[i].reshape(1, gmlp_w), ln_b=gmlp_ln_b[i].reshape(1, gmlp_w),
            ws=gmlp_ws[i].astype(BF16),
            bs_b=jnp.broadcast_to(gmlp_bs[i][:, :, None], gmlp_bs[i].shape + (LANES,)),
            conv_w8=jnp.pad(conv_w[i], ((0, SUBLANES - D_CONV), (0, 0))),
            conv_b=conv_b[i].reshape(1, xbc_w),
            dtb_row=jnp.pad(dt_bias[i].reshape(1, 2 * n_heads), lane_pad),
            a_row=jnp.pad(a, lane_pad),
            dsk=jnp.repeat(d_skip[i], HEAD_DIM).reshape(1, inner),
            ssd_g=ssd_norm_g[i].reshape(1, inner),
            pool_w=pool_w[i].astype(BF16), pool_scale=pool_scale[i],
            wbs=w_br_ssd[i].astype(BF16), wbp=w_br_pool[i].astype(BF16),
            wbg=w_br_gmlp[i].astype(BF16), wo=w_o[i].astype(BF16),
            wr=jnp.concatenate([wr.astype(BF16),
                                (wr - wr.astype(BF16).astype(F32)).astype(BF16)], axis=1),
            br=br,
            w_e_in=w_e_in[i].astype(BF16), w_e_out=w_e_out[i].astype(BF16),
        )
        ctx_new, hf_c, hb_c = _stream_layer(ctx, mod_ctx, lw, zero_state, zero_state,
                                            width=ctx.shape[1], final=False, g_final=g_final_row,
                                            need_tokens=not last)
        x, _, _ = _stream_layer(x, mod_lat, lw, hf_c, hb_c, width=GRID_W, final=last,
                                g_final=g_final_row)
        if not last:
            ctx = ctx_new
    return x
```

```python
import functools

import numpy as np
import jax
import jax.numpy as jnp
from jax import lax
from jax.experimental import pallas as pl
from jax.experimental.pallas import tpu as pltpu

F32 = jnp.float32
BF16 = jnp.bfloat16
HIGHEST = lax.Precision.HIGHEST

LANES = 128
SUBLANES = 8
VMEM_LIMIT_BYTES = 56 * 1024 * 1024

HEAD_DIM = 64
SSD_GROUPS = 4
SSD_STATE = 128
CHUNK = 128
D_CONV = 5
POOL_WINDOWS = (2, 4, 8, 16)
GMLP_GROUPS = 4
MOE_GROUPS = 4
EXPERTS_PER_GROUP = 4
N_PAIRS = 6
N_CLASSES = MOE_GROUPS * N_PAIRS
LOG2_E = 1.4426950408889634
RMS_EPS = 1e-6
LN_EPS = 1e-5
SSD_NORM_EPS = 1e-5
GRID_W = 64

LANE_W_LO = 32
LANE_W_HI = 33

EXPERT_TILE = 512
SMALL_EXPERT_TILE = 128
SMALL_STREAM_TOKENS = 4096
ROW_TILE = 2048
ROW_UNROLL = 8
MATMUL_TILE = 512
SSD_CHUNKS_PER_STEP = 8


def _params(*sem):
    return pltpu.CompilerParams(dimension_semantics=sem, vmem_limit_bytes=VMEM_LIMIT_BYTES)


def _sigmoid(v):
    return 0.5 * (1.0 + jnp.tanh(0.5 * v))


def _silu(v):
    return v * _sigmoid(v)


def _resident(shape):
    return pl.BlockSpec(shape, lambda *_: (0,) * len(shape), pipeline_mode=pl.Buffered(1))


def _modnorm(x, g, shift, scale):
    y = x * lax.rsqrt(jnp.mean(x * x, axis=-1, keepdims=True) + RMS_EPS)
    return y * g * (1.0 + scale) + shift


def _mod_kernel(c_ref, w_ref, b_ref, o_ref):
    s = _silu(c_ref[...])
    o_ref[0] = jnp.dot(s, w_ref[0], precision=HIGHEST, preferred_element_type=F32) + b_ref[0]


def _modulation(rows, w_mod, b_mod):
    depth, d, n = w_mod.shape
    tn = 1536
    return pl.pallas_call(
        _mod_kernel,
        grid=(depth, n // tn),
        in_specs=[
            pl.BlockSpec((SUBLANES, d), lambda i, j: (0, 0)),
            pl.BlockSpec((1, d, tn), lambda i, j: (i, 0, j)),
            pl.BlockSpec((1, 1, tn), lambda i, j: (i, 0, j)),
        ],
        out_specs=pl.BlockSpec((1, SUBLANES, tn), lambda i, j: (i, 0, j)),
        out_shape=jax.ShapeDtypeStruct((depth, SUBLANES, n), F32),
        compiler_params=_params("parallel", "parallel"),
        name="modulation",
    )(rows, w_mod, b_mod.reshape(depth, 1, n))


def _inproj_kernel(x_ref, xp_ref, xn_ref, mod_ref, g_ref, wx_ref, wpg_ref, wdt_ref, cw_ref, cb_ref,
                   lng_ref, lnb_ref, ws_ref, bs_ref, xab_ref, dt_ref, pp_ref, ygm_ref, xw,
                   *, pool_w, gmlp_w):
    j = pl.program_id(1)
    tm = x_ref.shape[1]

    def normed(v):
        return _modnorm(v, g_ref[...], mod_ref[0, 0:1, :], mod_ref[0, 1:2, :]).astype(BF16)

    h = normed(x_ref[0])
    pv = jnp.where(j > 0, 1.0, 0.0)
    nv = jnp.where(j < pl.num_programs(1) - 1, 1.0, 0.0)
    xc_prev = jnp.dot(normed(xp_ref[0]), wx_ref[...], preferred_element_type=F32) * pv
    xc_cur = jnp.dot(h, wx_ref[...], preferred_element_type=F32)
    xc_next = jnp.dot(normed(xn_ref[0]), wx_ref[...], preferred_element_type=F32) * nv
    dt_ref[0] = jnp.dot(h, wdt_ref[...], preferred_element_type=F32)
    pg = jnp.dot(h, wpg_ref[...], preferred_element_type=F32)
    for s in range(xw.shape[0]):
        cols = slice(s * LANES, (s + 1) * LANES)
        xw[s, 0:SUBLANES, :] = xc_prev[:, cols]
        xw[s, SUBLANES:SUBLANES + tm, :] = xc_cur[:, cols]
        xw[s, SUBLANES + tm:, :] = xc_next[:, cols]
    for s in range(xw.shape[0]):
        cols = slice(s * LANES, (s + 1) * LANES)
        for rb in range(tm // CHUNK):
            r0 = SUBLANES + rb * CHUNK - D_CONV // 2
            acc = cb_ref[:, cols] + cw_ref[0:1, cols] * xw[s, r0:r0 + CHUNK, :]
            for k in range(1, D_CONV):
                acc = acc + cw_ref[k:k + 1, cols] * xw[s, r0 + k:r0 + k + CHUNK, :]
            xab_ref[0, rb * CHUNK:(rb + 1) * CHUNK, cols] = _silu(acc).astype(BF16)

    pp_ref[0] = pg[:, :pool_w]
    uv = pg[:, pool_w:]
    act = uv * (0.5 * (1.0 + jnp.tanh(0.7978845608028654 * (uv + 0.044715 * (uv * uv * uv)))))
    u = act[:, :gmlp_w]
    v = act[:, gmlp_w:]
    mu = jnp.mean(v, axis=-1, keepdims=True)
    vc = v - mu
    var = jnp.mean(vc * vc, axis=-1, keepdims=True)
    vn = (vc * lax.rsqrt(var + LN_EPS) * lng_ref[...] + lnb_ref[...]).astype(BF16)
    gw = gmlp_w // GMLP_GROUPS
    for c in range(tm // CHUNK):
        rows = slice(c * CHUNK, (c + 1) * CHUNK)
        for g in range(GMLP_GROUPS):
            cols = slice(g * gw, (g + 1) * gw)
            mixed = jnp.dot(ws_ref[g], vn[rows, cols], preferred_element_type=F32) + bs_ref[g]
            ygm_ref[0, rows, cols] = (u[rows, cols] * mixed).astype(BF16)


def _inproj(x, modv, g_norm, wx, wpg, wdt, conv_w8, conv_b, ln_g, ln_b, ws, bs_b, *, pool_w, gmlp_w):
    b, l, d = x.shape
    tm = min(MATMUL_TILE, l)
    xbc_w = wx.shape[1]
    hb = tm // SUBLANES
    nhb = l // SUBLANES
    kern = functools.partial(_inproj_kernel, pool_w=pool_w, gmlp_w=gmlp_w)
    tok = lambda width: pl.BlockSpec((1, tm, width), lambda i, j: (i, j, 0))
    full = lambda a: _resident(a.shape)
    return pl.pallas_call(
        kern,
        grid=(b, l // tm),
        in_specs=[
            tok(d),
            pl.BlockSpec((1, SUBLANES, d), lambda i, j: (i, jnp.maximum(j * hb - 1, 0), 0)),
            pl.BlockSpec((1, SUBLANES, d), lambda i, j: (i, jnp.minimum(j * hb + hb, nhb - 1), 0)),
            pl.BlockSpec((1, SUBLANES, d), lambda i, j: (i, 0, 0)),
            full(g_norm), full(wx), full(wpg), full(wdt), full(conv_w8), full(conv_b),
            full(ln_g), full(ln_b), full(ws), full(bs_b),
        ],
        out_specs=[tok(xbc_w), tok(LANES), tok(pool_w), tok(gmlp_w)],
        out_shape=[
            jax.ShapeDtypeStruct((b, l, xbc_w), BF16),
            jax.ShapeDtypeStruct((b, l, LANES), F32),
            jax.ShapeDtypeStruct((b, l, pool_w), F32),
            jax.ShapeDtypeStruct((b, l, gmlp_w), BF16),
        ],
        scratch_shapes=[pltpu.VMEM((xbc_w // LANES, tm + 2 * SUBLANES, LANES), F32)],
        compiler_params=_params("parallel", "parallel"),
        name="inproj",
    )(x, x, x, modv, g_norm, wx, wpg, wdt, conv_w8, conv_b, ln_g, ln_b, ws, bs_b)


def _ssd_kernel(xab_ref, dt_ref, dtb_ref, a_ref, exp_ref, h0_ref, add_ref, y_ref, hout_ref, st,
                *, reverse, n_steps, chunks, inner):
    step = pl.program_id(1)
    bcw = SSD_GROUPS * SSD_STATE
    heads_per_group = inner // HEAD_DIM // SSD_GROUPS
    gp = heads_per_group * HEAD_DIM
    lane0 = (inner // HEAD_DIM) if reverse else 0

    @pl.when(step == 0)
    def _():
        st[...] = h0_ref[0]

    ii = lax.broadcasted_iota(jnp.int32, (CHUNK, CHUNK), 0)
    jj = lax.broadcasted_iota(jnp.int32, (CHUNK, CHUNK), 1)
    causal = (ii <= jj) if reverse else (ii >= jj)
    tri = causal.astype(BF16)
    last_row = 0 if reverse else CHUNK - 1
    head_of_lane = lax.broadcasted_iota(jnp.int32, (CHUNK, gp), 1) // HEAD_DIM

    def split(v, terms):
        out = []
        for _ in range(terms - 1):
            out.append(v.astype(BF16))
            v = v - out[-1].astype(F32)
        return out + [v.astype(BF16)]

    order = range(chunks - 1, -1, -1) if reverse else range(chunks)
    dtms, parts = {}, []
    for ci in order:
        z = dt_ref[0, ci * CHUNK:(ci + 1) * CHUNK, :] + dtb_ref[...]
        dtms[ci] = jnp.maximum(z, 0.0) + jnp.log(1.0 + jnp.exp(-jnp.abs(z)))
        parts += split(dtms[ci] * a_ref[...], 3)
    cums = jnp.dot(tri, jnp.concatenate(parts, axis=1), preferred_element_type=F32)
    acums, lhs = {}, []
    for n, ci in enumerate(order):
        c3 = cums[:, 3 * n * LANES:3 * (n + 1) * LANES]
        acum = c3[:, :LANES] + c3[:, LANES:2 * LANES] + c3[:, 2 * LANES:]
        acums[ci] = acum
        alast = acum[last_row:last_row + 1, :]
        lhs.append(jnp.concatenate(split(jnp.exp(acum), 2), axis=1))
        lhs.append(jnp.concatenate(split(dtms[ci] * jnp.exp(alast - acum), 2), axis=1))
    per_channel = jnp.dot(jnp.concatenate(lhs, axis=0), exp_ref[...], preferred_element_type=F32)
    coeffs = {}
    for n, ci in enumerate(order):
        r0 = 2 * n * CHUNK
        e2 = acums[ci] * LOG2_E
        coeffs[ci] = (e2, (e2 - jnp.log2(dtms[ci])).T, per_channel[r0:r0 + CHUNK],
                      per_channel[r0 + CHUNK:r0 + 2 * CHUNK])

    for ci in order:
        rows = slice(ci * CHUNK, (ci + 1) * CHUNK)
        e2, src_t, eac_c, wst_c = coeffs[ci]

        def cols_f32(lo, hi):
            return xab_ref[0, rows, lo:hi].astype(F32)

        def cols_bf16(lo, hi):
            return xab_ref[0, rows, lo:hi]

        for g in range(SSD_GROUPS):
            b0 = inner + g * SSD_STATE
            c0 = inner + bcw + g * SSD_STATE
            gcols = slice(g * gp, (g + 1) * gp)
            cg = cols_bf16(c0, c0 + SSD_STATE)
            bgt = cols_f32(b0, b0 + SSD_STATE).T.astype(BF16)
            cbm = jnp.dot(cg, bgt, preferred_element_type=F32)
            sg = st[g]
            yoff = jnp.dot(cg, sg.astype(BF16), preferred_element_type=F32)
            xs_f = cols_f32(g * gp, (g + 1) * gp)
            xs_b = cols_bf16(g * gp, (g + 1) * gp)
            ms, xbd = [], []
            for k in range(heads_per_group):
                ln = lane0 + g * heads_per_group + k
                seg = e2[:, ln:ln + 1] - src_t[ln:ln + 1, :]
                ms.append((cbm * jnp.exp2(jnp.where(causal, seg, -jnp.inf))).astype(BF16))
                xbd.append(jnp.where(head_of_lane == k, xs_b, jnp.zeros_like(xs_b)))
            yd = jnp.dot(jnp.concatenate(ms, axis=1), jnp.concatenate(xbd, axis=0),
                         preferred_element_type=F32)
            yg = yd + yoff * eac_c[:, gcols]
            if reverse:
                yg = yg + add_ref[0, rows, gcols]
            else:
                yg = yg + add_ref[:, gcols] * xs_f
            y_ref[0, rows, gcols] = yg
            xd = (xs_f * wst_c[:, gcols]).astype(BF16)
            st[g] = (sg * eac_c[last_row:last_row + 1, gcols]
                     + jnp.dot(bgt, xd, preferred_element_type=F32))

    @pl.when(step == n_steps - 1)
    def _():
        hout_ref[0] = st[...]


def _head_expand(inner, lane0):
    e = np.zeros((LANES, inner), np.float32)
    for h in range(inner // HEAD_DIM):
        e[lane0 + h, h * HEAD_DIM:(h + 1) * HEAD_DIM] = 1.0
    return jnp.asarray(np.concatenate([e, e], axis=0), BF16)


def _ssd_scan(xab, dtr, dtb_row, a_row, h0, add, *, reverse, inner):
    b, l, xw_cols = xab.shape
    chunks = min(SSD_CHUNKS_PER_STEP, l // CHUNK)
    n_steps = l // (chunks * CHUNK)
    gp = inner // SSD_GROUPS
    block_of = (lambda j: n_steps - 1 - j) if reverse else (lambda j: j)
    cur = lambda width: pl.BlockSpec((1, chunks * CHUNK, width), lambda i, j: (i, block_of(j), 0))
    state = pl.BlockSpec((1, SSD_GROUPS, SSD_STATE, gp), lambda i, j: (i, 0, 0, 0))
    lane0 = (inner // HEAD_DIM) if reverse else 0
    kern = functools.partial(_ssd_kernel, reverse=reverse, n_steps=n_steps, chunks=chunks,
                             inner=inner)
    return pl.pallas_call(
        kern,
        grid=(b, n_steps),
        in_specs=[cur(xw_cols), cur(LANES), _resident((1, LANES)), _resident((1, LANES)),
                  _resident((2 * LANES, inner)), state,
                  cur(inner) if reverse else _resident((1, inner))],
        out_specs=[cur(inner), state],
        out_shape=[jax.ShapeDtypeStruct((b, l, inner), F32),
                   jax.ShapeDtypeStruct((b, SSD_GROUPS, SSD_STATE, gp), F32)],
        scratch_shapes=[pltpu.VMEM((SSD_GROUPS, SSD_STATE, gp), F32)],
        compiler_params=_params("parallel", "arbitrary"),
        name="ssd_bwd" if reverse else "ssd_fwd",
    )(xab, dtr, dtb_row, a_row, _head_expand(inner, lane0), h0, add)


def _pool_kernel(pp_ref, band_ref, cinv_ref, inv_ref, pw_ref, ps_ref, o_ref, cp, *, width, rows,
                 slab):
    g = pl.program_id(1)
    pad = SUBLANES * width
    l = rows * width
    rows_per_slab = slab // width

    def body(window):
        lo = window // 2
        hi = window - lo
        cp[0:pad, :] = jnp.zeros((pad, LANES), F32)
        cp[pad + l:, :] = jnp.zeros((pad, LANES), F32)

        def col_pool(s, carry):
            off = pl.multiple_of(s * slab, slab)
            v = pp_ref[0, pl.ds(off, slab), :]
            v_hi = v.astype(BF16)
            v_lo = (v - v_hi.astype(F32)).astype(BF16)
            sums = jnp.dot(band_ref[0], jnp.concatenate([v_hi, v_lo], axis=1),
                           preferred_element_type=F32)
            cp[pl.ds(pad + off, slab), :] = (sums[:, :LANES] + sums[:, LANES:]) * cinv_ref[0]
            return carry

        lax.fori_loop(0, l // slab, col_pool, 0, unroll=min(4, l // slab))

        def row_pool(s, carry):
            off = pl.multiple_of(s * slab, slab)
            parts = []
            for r in range(rows_per_slab):
                base = pad + off + r * width
                acc = cp[pl.ds(base - lo * width, width), :]
                for k in range(1 - lo, hi):
                    acc = acc + cp[pl.ds(base + k * width, width), :]
                parts.append(acc * inv_ref[0, pl.ds(s * rows_per_slab + r, 1), :])
            pooled = parts[0] if len(parts) == 1 else jnp.concatenate(parts, axis=0)
            d = (pooled - pp_ref[0, pl.ds(off, slab), :]).astype(BF16)
            y = jnp.dot(d, pw_ref[0], preferred_element_type=F32) * ps_ref[0]
            o_ref[0, pl.ds(off, slab), :] = y.astype(BF16)
            return carry

        lax.fori_loop(0, l // slab, row_pool, 0, unroll=min(4, l // slab))

    for gi, window in enumerate(POOL_WINDOWS):
        pl.when(g == gi)(functools.partial(body, window))


def _pool_tables(l, width):
    rows = l // width
    slab = max(2 * CHUNK, width)
    n_g = len(POOL_WINDOWS)
    band = np.zeros((n_g, slab, slab), np.float32)
    cinv = np.ones((n_g, slab, LANES), np.float32)
    inv = np.ones((n_g, max(rows, SUBLANES), LANES), np.float32)
    for gi, w in enumerate(POOL_WINDOWS):
        lo, hi = w // 2, w - w // 2
        for t in range(slab):
            r, c = divmod(t, width)
            c0, c1 = max(c - lo, 0), min(c + hi, width)
            band[gi, t, r * width + c0:r * width + c1] = 1.0
            cinv[gi, t, :] = 1.0 / (c1 - c0)
        for r in range(rows):
            inv[gi, r, :] = 1.0 / (min(r + hi, rows) - max(r - lo, 0))
    return jnp.asarray(band, BF16), jnp.asarray(cinv), jnp.asarray(inv), rows, slab


def _pool_branch(pp, pool_w, pool_scale, *, width):
    b, l, pw = pp.shape
    n_g = len(POOL_WINDOWS)
    band, cinv, inv, rows, slab = _pool_tables(l, width)
    kern = functools.partial(_pool_kernel, width=width, rows=rows, slab=slab)
    return pl.pallas_call(
        kern,
        grid=(b, n_g),
        in_specs=[
            pl.BlockSpec((1, l, LANES), lambda i, g: (i, 0, g)),
            pl.BlockSpec((1, slab, slab), lambda i, g: (g, 0, 0)),
            pl.BlockSpec((1, slab, LANES), lambda i, g: (g, 0, 0)),
            pl.BlockSpec((1,) + inv.shape[1:], lambda i, g: (g, 0, 0)),
            pl.BlockSpec((1, LANES, LANES), lambda i, g: (g, 0, 0)),
            pl.BlockSpec((1, 1, LANES), lambda i, g: (g, 0, 0)),
        ],
        out_specs=pl.BlockSpec((1, l, LANES), lambda i, g: (i, 0, g)),
        out_shape=jax.ShapeDtypeStruct((b, l, pw), BF16),
        scratch_shapes=[pltpu.VMEM((l + 2 * SUBLANES * width, LANES), F32)],
        compiler_params=_params("parallel", "parallel"),
        name="pool",
    )(pp, band, cinv, inv, pool_w, pool_scale.reshape(n_g, 1, LANES))


def _mix_kernel(x_ref, mod_ref, g1_ref, g2_ref, wz_ref, wg_ref, y_ref, sg_ref, yp_ref, ygm_ref,
                wbs_ref, wbp_ref, wbg_ref, wo_ref, wr_ref, br_ref, x1_ref, h2_ref):
    x = x_ref[0]
    d = x.shape[1]
    h = _modnorm(x, g1_ref[...], mod_ref[0, 0:1, :], mod_ref[0, 1:2, :]).astype(BF16)
    z = jnp.dot(h, wz_ref[...], preferred_element_type=F32)
    t = y_ref[0] * _silu(z)
    ys = t * lax.rsqrt(jnp.mean(t * t, axis=-1, keepdims=True) + SSD_NORM_EPS) * sg_ref[...]
    gates = _sigmoid(jnp.dot(h, wg_ref[...], preferred_element_type=F32))
    merged = gates[:, :d] * jnp.dot(ys.astype(BF16), wbs_ref[...], preferred_element_type=F32)
    merged = merged + gates[:, d:2 * d] * jnp.dot(yp_ref[0], wbp_ref[...], preferred_element_type=F32)
    merged = merged + gates[:, 2 * d:] * jnp.dot(ygm_ref[0], wbg_ref[...], preferred_element_type=F32)
    mix = jnp.dot(merged.astype(BF16), wo_ref[...], preferred_element_type=F32)
    x1 = x + mod_ref[0, 2:3, :] * mix
    x1_ref[0] = x1
    h2 = _modnorm(x1, g2_ref[...], mod_ref[0, 3:4, :], mod_ref[0, 4:5, :])
    h2_ref[0, :, :d] = h2

    h2_hi = h2.astype(BF16)
    h2_lo = (h2 - h2_hi.astype(F32)).astype(BF16)
    p_hi = jnp.dot(h2_hi, wr_ref[...], preferred_element_type=F32)
    p_lo = jnp.dot(h2_lo, wr_ref[:, :LANES], preferred_element_type=F32)
    lg = p_hi[:, :LANES] + p_hi[:, LANES:] + p_lo + br_ref[...]
    lane = lax.broadcasted_iota(jnp.int32, lg.shape, 1).astype(F32)
    ninf = -jnp.inf
    big = float(LANES)
    gl = jnp.where(lane < MOE_GROUPS, lg, ninf)
    gmax = jnp.max(gl, axis=-1, keepdims=True)
    pg_top = 1.0 / jnp.sum(jnp.exp(gl - gmax), axis=-1, keepdims=True)
    gidx = jnp.min(jnp.where(gl == gmax, lane, big), axis=-1, keepdims=True)
    e0 = MOE_GROUPS + EXPERTS_PER_GROUP * gidx
    el = jnp.where((lane >= e0) & (lane < e0 + EXPERTS_PER_GROUP), lg, ninf)
    v1 = jnp.max(el, axis=-1, keepdims=True)
    i1 = jnp.min(jnp.where(el == v1, lane, big), axis=-1, keepdims=True)
    el2 = jnp.where(lane == i1, ninf, el)
    v2 = jnp.max(el2, axis=-1, keepdims=True)
    i2 = jnp.min(jnp.where(el2 == v2, lane, big), axis=-1, keepdims=True)
    r = jnp.exp(v2 - v1)
    wt1 = pg_top * (1.0 / (1.0 + r))
    wt2 = pg_top * (r / (1.0 + r))
    first_lo = i1 < i2
    lo = jnp.where(first_lo, i1, i2) - e0
    hi = jnp.where(first_lo, i2, i1) - e0
    w_lo = jnp.where(first_lo, wt1, wt2)
    w_hi = jnp.where(first_lo, wt2, wt1)
    cls = gidx * N_PAIRS + lo * (7.0 - lo) * 0.5 + (hi - lo - 1.0)
    info = jnp.where(lane == cls, 1.0, 0.0)
    info = jnp.where(lane == LANE_W_LO, w_lo, info)
    info = jnp.where(lane == LANE_W_HI, w_hi, info)
    h2_ref[0, :, d:] = info


def _mix(x, modv, g1, g2, wz, wg, y, ssd_g, yp, ygm, wbs, wbp, wbg, wo, wr, br):
    b, l, d = x.shape
    tm = min(MATMUL_TILE, l)
    tok = lambda width: pl.BlockSpec((1, tm, width), lambda i, j: (i, j, 0))
    full = lambda a: _resident(a.shape)
    return pl.pallas_call(
        _mix_kernel,
        grid=(b, l // tm),
        in_specs=[
            tok(d), pl.BlockSpec((1, SUBLANES, d), lambda i, j: (i, 0, 0)),
            full(g1), full(g2), full(wz), full(wg), tok(y.shape[2]), full(ssd_g),
            tok(yp.shape[2]), tok(ygm.shape[2]), full(wbs), full(wbp), full(wbg), full(wo),
            full(wr), full(br),
        ],
        out_specs=[tok(d), tok(d + LANES)],
        out_shape=[jax.ShapeDtypeStruct((b, l, d), F32),
                   jax.ShapeDtypeStruct((b, l, d + LANES), F32)],
        compiler_params=_params("parallel", "parallel"),
        name="mix",
    )(x, modv, g1, g2, wz, wg, y, ssd_g, yp, ygm, wbs, wbp, wbg, wo, wr, br)


def _sort_kernel(info_ref, pos_ref, tcls_ref, cnt, base, *, tile, n_slabs, n_tile_lanes):
    phase = pl.program_id(0)
    i = pl.program_id(1)
    ts = info_ref.shape[0]
    sub = lax.broadcasted_iota(jnp.int32, (LANES, LANES), 0)
    lan = lax.broadcasted_iota(jnp.int32, (LANES, LANES), 1)

    @pl.when((phase == 0) & (i == 0))
    def _():
        cnt[...] = jnp.zeros_like(cnt)

    @pl.when(phase == 0)
    def _():
        lane = lax.broadcasted_iota(jnp.int32, (ts, LANES), 1)
        one_hot = jnp.where(lane < N_CLASSES, info_ref[...], 0.0)
        cnt[...] += jnp.broadcast_to(jnp.sum(one_hot, axis=0, keepdims=True), cnt.shape)

    @pl.when((phase == 1) & (i == 0))
    def _():
        counts = cnt[...].T
        padded = jnp.ceil(counts * (1.0 / tile)) * tile
        starts = jnp.dot((lan < sub).astype(F32), padded, precision=HIGHEST,
                         preferred_element_type=F32)
        base[...] = starts
        tcls_ref[...] = jnp.zeros_like(tcls_ref)
        ends = (starts + padded)[:, 0:1]
        tl = lax.broadcasted_iota(jnp.int32, (LANES, n_tile_lanes), 1).astype(F32) * tile
        cs = lax.broadcasted_iota(jnp.int32, (LANES, n_tile_lanes), 0)
        done = jnp.where((cs < N_CLASSES) & (ends <= tl), 1.0, 0.0)
        tc = jnp.minimum(jnp.sum(done, axis=0, keepdims=True), N_CLASSES - 1.0)
        total = jnp.max(jnp.where(cs < N_CLASSES, ends, 0.0), axis=0, keepdims=True)
        valid = jnp.where(tl[0:1, :] < total, 1.0, 0.0)
        tcls_ref[0, 0:1, :] = tc.astype(jnp.int32)
        tcls_ref[0, 1:2, :] = valid.astype(jnp.int32)
        last_tile = jnp.where((sub < N_CLASSES) & (padded > 0.0), starts + padded - tile, -1.0)
        in_use = jnp.max(jnp.where(sub < N_CLASSES, starts + padded, 0.0), axis=0, keepdims=True)
        last_tile = jnp.where(sub == N_CLASSES, in_use, last_tile)
        tcls_ref[0, 2:3, 0:LANES] = last_tile.T[0:1, :].astype(jnp.int32)

    @pl.when(phase == 1)
    def _():
        oh_t = jnp.concatenate(
            [info_ref[k * LANES:(k + 1) * LANES, :].T for k in range(n_slabs)], axis=1)
        csub = lax.broadcasted_iota(jnp.int32, oh_t.shape, 0)
        oh_t = jnp.where(csub < N_CLASSES, oh_t, 0.0)
        ti = lax.broadcasted_iota(jnp.int32, (ts, ts), 0)
        tj = lax.broadcasted_iota(jnp.int32, (ts, ts), 1)
        before = jnp.dot(oh_t.astype(BF16), (ti < tj).astype(BF16), preferred_element_type=F32)
        posv = jnp.sum(oh_t * (before + base[:, 0:1]), axis=0, keepdims=True)
        pos_ref[0] = posv.astype(jnp.int32)
        base[...] += jnp.broadcast_to(jnp.sum(oh_t, axis=-1, keepdims=True), base.shape)


def _route_sort(h2ext, *, d, n_exp_tiles, tile):
    t = h2ext.shape[0]
    ts = min(1024, t)
    nt = t // ts
    n_tile_lanes = -(-n_exp_tiles // LANES) * LANES
    kern = functools.partial(_sort_kernel, tile=tile, n_slabs=ts // LANES,
                             n_tile_lanes=n_tile_lanes)
    pos, tcls = pl.pallas_call(
        kern,
        grid=(2, nt),
        in_specs=[pl.BlockSpec((ts, LANES), lambda p, i: (i, d // LANES))],
        out_specs=[pl.BlockSpec((1, 1, ts), lambda p, i: (i * p, 0, 0)),
                   pl.BlockSpec((1, SUBLANES, n_tile_lanes), lambda p, i: (0, 0, 0))],
        out_shape=[jax.ShapeDtypeStruct((nt, 1, ts), jnp.int32),
                   jax.ShapeDtypeStruct((1, SUBLANES, n_tile_lanes), jnp.int32)],
        scratch_shapes=[pltpu.VMEM((LANES, LANES), F32), pltpu.VMEM((LANES, LANES), F32)],
        compiler_params=_params("arbitrary", "arbitrary"),
        name="route_sort",
    )(h2ext)
    return (pos.reshape(t), tcls[0, 0, :n_exp_tiles], tcls[0, 1, :n_exp_tiles],
            tcls[0, 2, :N_CLASSES + 1])


def _row_copies(n_groups, make_copy):
    def issue(g, carry):
        for j in range(SUBLANES):
            make_copy(g, j).start(priority=j % 2)
        return carry

    lax.fori_loop(0, n_groups, issue, 0)


def _row_of(ref, r):
    return ref.at[lax.shift_right_logical(r, 3), pl.ds(lax.bitwise_and(r, SUBLANES - 1), 1)]


def _dispatch_kernel(pad_ref, pos_ref, rows_ref, out_ref, zeros, sem, zsem):
    n_groups = rows_ref.shape[0]

    @pl.when(pl.program_id(0) == 0)
    def _():
        zeros[...] = jnp.zeros_like(zeros)
        n_rows = out_ref.shape[0] * SUBLANES
        tile_groups = zeros.shape[0]
        tile = tile_groups * SUBLANES
        starts = [pad_ref[c] for c in range(N_CLASSES)]
        starts += [pad_ref[N_CLASSES] + k * tile for k in range(N_CLASSES)]
        wanted = [s >= 0 for s in starts[:N_CLASSES]] + [s < n_rows for s in starts[N_CLASSES:]]

        def fill(s):
            g0 = pl.multiple_of(
                lax.shift_right_logical(jnp.clip(s, 0, n_rows - tile), 3), tile_groups)
            return pltpu.make_async_copy(zeros, out_ref.at[pl.ds(g0, tile_groups)], zsem)

        for s, w in zip(starts, wanted):
            pl.when(w)(lambda s=s: fill(s).start())
        for s, w in zip(starts, wanted):
            pl.when(w)(lambda s=s: fill(s).wait())

    _row_copies(n_groups, lambda g, j: pltpu.make_async_copy(
        rows_ref.at[g, pl.ds(j, 1)], _row_of(out_ref, pos_ref[0, 0, g * SUBLANES + j]), sem))
    pltpu.make_async_copy(rows_ref, out_ref.at[pl.ds(0, n_groups)], sem).wait()


def _dispatch(h2ext, pos, pad_rows, *, n_sorted_rows, tile):
    t, w = h2ext.shape
    tm = min(ROW_TILE, t)
    nt = t // tm
    grid_spec = pltpu.PrefetchScalarGridSpec(
        num_scalar_prefetch=1,
        grid=(nt,),
        in_specs=[
            pl.BlockSpec((1, 1, tm), lambda i, pad: (i, 0, 0), memory_space=pltpu.SMEM),
            pl.BlockSpec((tm // SUBLANES, SUBLANES, w), lambda i, pad: (i, 0, 0)),
        ],
        out_specs=pl.BlockSpec(memory_space=pl.ANY),
        scratch_shapes=[pltpu.VMEM((tile // SUBLANES, SUBLANES, w), F32),
                        pltpu.SemaphoreType.DMA, pltpu.SemaphoreType.DMA],
    )
    out = pl.pallas_call(
        _dispatch_kernel,
        grid_spec=grid_spec,
        out_shape=jax.ShapeDtypeStruct((n_sorted_rows // SUBLANES, SUBLANES, w), F32),
        compiler_params=_params("arbitrary"),
        name="dispatch",
    )(pad_rows, pos.reshape(nt, 1, tm), h2ext.reshape(t // SUBLANES, SUBLANES, w))
    return out.reshape(n_sorted_rows, w)


def _expert_kernel(elo_ref, ehi_ref, valid_ref, rows_ref, wil_ref, wih_ref, wol_ref, woh_ref,
                   o_ref, *, d, d_expert):
    del elo_ref, ehi_ref
    i = pl.program_id(0)

    @pl.when(valid_ref[i] == 0)
    def _():
        o_ref[...] = jnp.zeros_like(o_ref)

    @pl.when(valid_ref[i] != 0)
    def _():
        h = rows_ref[:, :d].astype(BF16)
        info = rows_ref[:, d:]

        def expert(wi_ref, wo_ref):
            gu = jnp.dot(h, wi_ref[0], preferred_element_type=F32)
            hid = _silu(gu[:, :d_expert]) * gu[:, d_expert:]
            return jnp.dot(hid.astype(BF16), wo_ref[0], preferred_element_type=F32)

        o_ref[...] = (info[:, LANE_W_LO:LANE_W_LO + 1] * expert(wil_ref, wol_ref)
                      + info[:, LANE_W_HI:LANE_W_HI + 1] * expert(wih_ref, woh_ref))


def _experts(sorted_rows, e_lo, e_hi, valid, w_e_in, w_e_out, *, d, tile):
    n_rows, w = sorted_rows.shape
    tm = tile
    n_tiles = n_rows // tm
    d_expert = w_e_out.shape[1]
    kern = functools.partial(_expert_kernel, d=d, d_expert=d_expert)
    grid_spec = pltpu.PrefetchScalarGridSpec(
        num_scalar_prefetch=3,
        grid=(n_tiles,),
        in_specs=[
            pl.BlockSpec((tm, w), lambda i, lo, hi, v: (i * v[i], 0)),
            pl.BlockSpec((1,) + w_e_in.shape[1:], lambda i, lo, hi, v: (lo[i], 0, 0)),
            pl.BlockSpec((1,) + w_e_in.shape[1:], lambda i, lo, hi, v: (hi[i], 0, 0)),
            pl.BlockSpec((1,) + w_e_out.shape[1:], lambda i, lo, hi, v: (lo[i], 0, 0)),
            pl.BlockSpec((1,) + w_e_out.shape[1:], lambda i, lo, hi, v: (hi[i], 0, 0)),
        ],
        out_specs=pl.BlockSpec((tm, d), lambda i, lo, hi, v: (i, 0)),
    )
    return pl.pallas_call(
        kern,
        grid_spec=grid_spec,
        out_shape=jax.ShapeDtypeStruct((n_rows, d), F32),
        compiler_params=_params("arbitrary"),
        name="experts",
    )(e_lo, e_hi, valid, sorted_rows, w_e_in, w_e_in, w_e_out, w_e_out)


def _combine_kernel(pos_ref, x1_ref, mod_ref, gf_ref, moe_ref, o_ref, rows, sem, *, final):
    n_groups = rows.shape[0]
    _row_copies(n_groups, lambda g, j: pltpu.make_async_copy(
        _row_of(moe_ref, pos_ref[0, 0, g * SUBLANES + j]), rows.at[g, pl.ds(j, 1)], sem))
    pltpu.make_async_copy(moe_ref.at[pl.ds(0, n_groups)], rows, sem).wait()
    x2 = x1_ref[...] + mod_ref[0, 5:6, :] * rows[...]
    if final:
        x2 = x2 * lax.rsqrt(jnp.mean(x2 * x2, axis=-1, keepdims=True) + RMS_EPS) * gf_ref[...]
    o_ref[...] = x2


def _combine(x1, pos, modv, g_final, moe_sorted, *, seq_len, final):
    t, d = x1.shape
    tm = min(ROW_TILE, seq_len)
    nt = t // tm
    tiles_per_seq = seq_len // tm
    kern = functools.partial(_combine_kernel, final=final)
    grouped = lambda a: a.reshape(a.shape[0] // SUBLANES, SUBLANES, a.shape[1])
    tile = pl.BlockSpec((tm // SUBLANES, SUBLANES, d), lambda i: (i, 0, 0))
    out = pl.pallas_call(
        kern,
        grid=(nt,),
        in_specs=[
            pl.BlockSpec((1, 1, tm), lambda i: (i, 0, 0), memory_space=pltpu.SMEM),
            tile,
            pl.BlockSpec((1, SUBLANES, d), lambda i: (i // tiles_per_seq, 0, 0)),
            pl.BlockSpec((1, d), lambda i: (0, 0)),
            pl.BlockSpec(memory_space=pl.ANY),
        ],
        out_specs=tile,
        out_shape=jax.ShapeDtypeStruct((t // SUBLANES, SUBLANES, d), F32),
        scratch_shapes=[pltpu.VMEM((tm // SUBLANES, SUBLANES, d), F32), pltpu.SemaphoreType.DMA],
        compiler_params=_params("arbitrary"),
        name="combine",
    )(pos.reshape(nt, 1, tm), grouped(x1), modv, g_final, grouped(moe_sorted))
    return out.reshape(t, d)


def _pair_tables():
    lo, hi = [], []
    for g in range(MOE_GROUPS):
        for a in range(EXPERTS_PER_GROUP):
            for b in range(a + 1, EXPERTS_PER_GROUP):
                lo.append(g * EXPERTS_PER_GROUP + a)
                hi.append(g * EXPERTS_PER_GROUP + b)
    return jnp.asarray(lo, jnp.int32), jnp.asarray(hi, jnp.int32)


def _stream_layer(x, modv, lw, h0f, h0b, *, width, final, g_final, need_tokens=True):
    b, l, d = x.shape
    inner = lw["inner"]
    xab, dtr, pp, ygm = _inproj(x, modv, lw["g1"], lw["wx"], lw["wpg"], lw["wdt"], lw["conv_w8"],
                                lw["conv_b"], lw["ln_g"], lw["ln_b"], lw["ws"], lw["bs_b"],
                                pool_w=lw["pool_cols"], gmlp_w=lw["gmlp_w"])
    yf, hf = _ssd_scan(xab, dtr, lw["dtb_row"], lw["a_row"], h0f, lw["dsk"], reverse=False,
                       inner=inner)
    y, hb = _ssd_scan(xab, dtr, lw["dtb_row"], lw["a_row"], h0b, yf, reverse=True, inner=inner)
    if not need_tokens:
        return None, hf, hb
    yp = _pool_branch(pp, lw["pool_w"], lw["pool_scale"], width=width)
    x1, h2ext = _mix(x, modv, lw["g1"], lw["g2"], lw["wz"], lw["wg"], y, lw["ssd_g"], yp, ygm,
                     lw["wbs"], lw["wbp"], lw["wbg"], lw["wo"], lw["wr"], lw["br"])
    t = b * l
    h2ext = h2ext.reshape(t, d + LANES)
    tile = EXPERT_TILE if t >= SMALL_STREAM_TOKENS else SMALL_EXPERT_TILE
    n_exp_tiles = t // tile + N_CLASSES
    pos, tcls, valid, pad_rows = _route_sort(h2ext, d=d, n_exp_tiles=n_exp_tiles, tile=tile)
    pair_lo, pair_hi = _pair_tables()
    sorted_rows = _dispatch(h2ext, pos, pad_rows, n_sorted_rows=n_exp_tiles * tile, tile=tile)
    moe_sorted = _experts(sorted_rows, pair_lo[tcls], pair_hi[tcls], valid, lw["w_e_in"],
                          lw["w_e_out"], d=d, tile=tile)
    x2 = _combine(x1.reshape(t, d), pos, modv, g_final, moe_sorted, seq_len=l, final=final)
    return x2.reshape(b, l, d), hf, hb


def kernel(x, c, ctx, c_ctx, w_mod, b_mod, g_norm1, g_norm2, w_in, conv_w, conv_b, dt_bias, a_log,
           d_skip, ssd_norm_g, pool_w, pool_scale, gmlp_ln_g, gmlp_ln_b, gmlp_ws, gmlp_bs, w_br_ssd,
           w_br_pool, w_br_gmlp, w_o, w_rg, b_rg, w_re, b_re, w_e_in, w_e_out, g_final):
    b, seq, d = x.shape
    depth = w_mod.shape[0]
    n_heads = d_skip.shape[1]
    inner = n_heads * HEAD_DIM
    bcw = SSD_GROUPS * SSD_STATE
    xbc_w = inner + 2 * bcw
    pool_cols = pool_w.shape[1] * pool_w.shape[2]
    gmlp_w = gmlp_ln_g.shape[1]
    off_xbc = inner
    off_dt = off_xbc + xbc_w
    off_pool = off_dt + 2 * n_heads
    off_gmlp = off_pool + pool_cols
    off_gate = off_gmlp + 2 * gmlp_w

    rows = jnp.zeros((SUBLANES, d), F32).at[:b].set(c).at[b].set(c_ctx)
    mod = _modulation(rows, w_mod, b_mod)
    gp = inner // SSD_GROUPS
    zero_state = jnp.zeros((b, SSD_GROUPS, SSD_STATE, gp), F32)
    g_final_row = g_final.reshape(1, d)

    def pad_rows(m):
        return jnp.pad(m.reshape(m.shape[0], 6, d), ((0, 0), (0, 2), (0, 0)))

    for i in range(depth):
        last = i == depth - 1
        mod_lat = pad_rows(mod[i, :b])
        mod_ctx = pad_rows(jnp.broadcast_to(mod[i, b:b + 1], (b, 6 * d)))
        wi = w_in[i].astype(BF16)
        a = -jnp.exp(a_log[i].astype(F32)).reshape(1, 2 * n_heads)
        lane_pad = ((0, 0), (0, LANES - 2 * n_heads))
        wr = jnp.pad(jnp.concatenate([w_rg[i], w_re[i]], axis=1),
                     ((0, 0), (0, LANES - MOE_GROUPS - MOE_GROUPS * EXPERTS_PER_GROUP)))
        br = jnp.pad(jnp.concatenate([b_rg[i], b_re[i]]).reshape(1, -1),
                     ((0, 0), (0, LANES - MOE_GROUPS - MOE_GROUPS * EXPERTS_PER_GROUP)))
        lw = dict(
            inner=inner, xbc_w=xbc_w, pool_cols=pool_cols, gmlp_w=gmlp_w,
            g1=g_norm1[i].reshape(1, d), g2=g_norm2[i].reshape(1, d),
            wx=wi[:, off_xbc:off_dt].astype(BF16), wpg=wi[:, off_pool:off_gate].astype(BF16),
            wdt=jnp.pad(wi[:, off_dt:off_pool], lane_pad).astype(BF16),
            wz=wi[:, :inner].astype(BF16), wg=wi[:, off_gate:].astype(BF16),
            ln_g=gmlp_ln_g[i].reshape(1, gmlp_w), ln_b=gmlp_ln_b[i].reshape(1, gmlp_w),
            ws=gmlp_ws[i].astype(BF16),
            bs_b=jnp.broadcast_to(gmlp_bs[i][:, :, None], gmlp_bs[i].shape + (LANES,)),
            conv_w8=jnp.pad(conv_w[i], ((0, SUBLANES - D_CONV), (0, 0))),
            conv_b=conv_b[i].reshape(1, xbc_w),
            dtb_row=jnp.pad(dt_bias[i].reshape(1, 2 * n_heads), lane_pad),
            a_row=jnp.pad(a, lane_pad),
            dsk=jnp.repeat(d_skip[i], HEAD_DIM).reshape(1, inner),
            ssd_g=ssd_norm_g[i].reshape(1, inner),
            pool_w=pool_w[i].astype(BF16), pool_scale=pool_scale[i],
            wbs=w_br_ssd[i].astype(BF16), wbp=w_br_pool[i].astype(BF16),
            wbg=w_br_gmlp[i].astype(BF16), wo=w_o[i].astype(BF16),
            wr=jnp.concatenate([wr.astype(BF16),
                                (wr - wr.astype(BF16).astype(F32)).astype(BF16)], axis=1),
            br=br,
            w_e_in=w_e_in[i].astype(BF16), w_e_out=w_e_out[i].astype(BF16),
        )
        ctx_new, hf_c, hb_c = _stream_layer(ctx, mod_ctx, lw, zero_state, zero_state,
                                            width=ctx.shape[1], final=False, g_final=g_final_row,
                                            need_tokens=not last)
        x, _, _ = _stream_layer(x, mod_lat, lw, hf_c, hb_c, width=GRID_W, final=last,
                                g_final=g_final_row)
        if not last:
            ctx = ctx_new
    return x
```

```python
import functools

import numpy as np
import jax
import jax.numpy as jnp
from jax import lax
from jax.experimental import pallas as pl
from jax.experimental.pallas import tpu as pltpu

F32 = jnp.float32
BF16 = jnp.bfloat16
HIGHEST = lax.Precision.HIGHEST

LANES = 128
SUBLANES = 8
VMEM_LIMIT_BYTES = 56 * 1024 * 1024

HEAD_DIM = 64
SSD_GROUPS = 4
SSD_STATE = 128
CHUNK = 128
D_CONV = 5
POOL_WINDOWS = (2, 4, 8, 16)
GMLP_GROUPS = 4
MOE_GROUPS = 4
EXPERTS_PER_GROUP = 4
N_PAIRS = 6
N_CLASSES = MOE_GROUPS * N_PAIRS
LOG2_E = 1.4426950408889634
RMS_EPS = 1e-6
LN_EPS = 1e-5
SSD_NORM_EPS = 1e-5
GRID_W = 64

LANE_W_LO = 32
LANE_W_HI = 33

EXPERT_TILE = 512
SMALL_EXPERT_TILE = 128
SMALL_STREAM_TOKENS = 4096
ROW_TILE = 1024
ROW_UNROLL = 8
MATMUL_TILE = 512
SSD_CHUNKS_PER_STEP = 8


def _params(*sem):
    return pltpu.CompilerParams(dimension_semantics=sem, vmem_limit_bytes=VMEM_LIMIT_BYTES)


def _sigmoid(v):
    return 0.5 * (1.0 + jnp.tanh(0.5 * v))


def _silu(v):
    return v * _sigmoid(v)


def _resident(shape):
    return pl.BlockSpec(shape, lambda *_: (0,) * len(shape), pipeline_mode=pl.Buffered(1))


def _modnorm(x, g, shift, scale):
    y = x * lax.rsqrt(jnp.mean(x * x, axis=-1, keepdims=True) + RMS_EPS)
    return y * g * (1.0 + scale) + shift


def _mod_kernel(c_ref, w_ref, b_ref, o_ref):
    s = _silu(c_ref[...])
    o_ref[0] = jnp.dot(s, w_ref[0], precision=HIGHEST, preferred_element_type=F32) + b_ref[0]


def _modulation(rows, w_mod, b_mod):
    depth, d, n = w_mod.shape
    tn = 1536
    return pl.pallas_call(
        _mod_kernel,
        grid=(depth, n // tn),
        in_specs=[
            pl.BlockSpec((SUBLANES, d), lambda i, j: (0, 0)),
            pl.BlockSpec((1, d, tn), lambda i, j: (i, 0, j)),
            pl.BlockSpec((1, 1, tn), lambda i, j: (i, 0, j)),
        ],
        out_specs=pl.BlockSpec((1, SUBLANES, tn), lambda i, j: (i, 0, j)),
        out_shape=jax.ShapeDtypeStruct((depth, SUBLANES, n), F32),
        compiler_params=_params("parallel", "parallel"),
        name="modulation",
    )(rows, w_mod, b_mod.reshape(depth, 1, n))


def _inproj_kernel(x_ref, xp_ref, xn_ref, mod_ref, g_ref, wx_ref, wpg_ref, wdt_ref, cw_ref, cb_ref,
                   lng_ref, lnb_ref, ws_ref, bs_ref, xab_ref, dt_ref, pp_ref, ygm_ref, xw,
                   *, pool_w, gmlp_w):
    j = pl.program_id(1)
    tm = x_ref.shape[1]

    def normed(v):
        return _modnorm(v, g_ref[...], mod_ref[0, 0:1, :], mod_ref[0, 1:2, :]).astype(BF16)

    h = normed(x_ref[0])
    pv = jnp.where(j > 0, 1.0, 0.0)
    nv = jnp.where(j < pl.num_programs(1) - 1, 1.0, 0.0)
    xc_prev = jnp.dot(normed(xp_ref[0]), wx_ref[...], preferred_element_type=F32) * pv
    xc_cur = jnp.dot(h, wx_ref[...], preferred_element_type=F32)
    xc_next = jnp.dot(normed(xn_ref[0]), wx_ref[...], preferred_element_type=F32) * nv
    dt_ref[0] = jnp.dot(h, wdt_ref[...], preferred_element_type=F32)
    pg = jnp.dot(h, wpg_ref[...], preferred_element_type=F32)
    for s in range(xw.shape[0]):
        cols = slice(s * LANES, (s + 1) * LANES)
        xw[s, 0:SUBLANES, :] = xc_prev[:, cols]
        xw[s, SUBLANES:SUBLANES + tm, :] = xc_cur[:, cols]
        xw[s, SUBLANES + tm:, :] = xc_next[:, cols]
    for s in range(xw.shape[0]):
        cols = slice(s * LANES, (s + 1) * LANES)
        for rb in range(tm // CHUNK):
            r0 = SUBLANES + rb * CHUNK - D_CONV // 2
            acc = cb_ref[:, cols] + cw_ref[0:1, cols] * xw[s, r0:r0 + CHUNK, :]
            for k in range(1, D_CONV):
                acc = acc + cw_ref[k:k + 1, cols] * xw[s, r0 + k:r0 + k + CHUNK, :]
            xab_ref[0, rb * CHUNK:(rb + 1) * CHUNK, cols] = _silu(acc).astype(BF16)

    pp_ref[0] = pg[:, :pool_w]
    uv = pg[:, pool_w:]
    act = uv * (0.5 * (1.0 + jnp.tanh(0.7978845608028654 * (uv + 0.044715 * (uv * uv * uv)))))
    u = act[:, :gmlp_w]
    v = act[:, gmlp_w:]
    mu = jnp.mean(v, axis=-1, keepdims=True)
    vc = v - mu
    var = jnp.mean(vc * vc, axis=-1, keepdims=True)
    vn = (vc * lax.rsqrt(var + LN_EPS) * lng_ref[...] + lnb_ref[...]).astype(BF16)
    gw = gmlp_w // GMLP_GROUPS
    for c in range(tm // CHUNK):
        rows = slice(c * CHUNK, (c + 1) * CHUNK)
        for g in range(GMLP_GROUPS):
            cols = slice(g * gw, (g + 1) * gw)
            mixed = jnp.dot(ws_ref[g], vn[rows, cols], preferred_element_type=F32) + bs_ref[g]
            ygm_ref[0, rows, cols] = (u[rows, cols] * mixed).astype(BF16)


def _inproj(x, modv, g_norm, wx, wpg, wdt, conv_w8, conv_b, ln_g, ln_b, ws, bs_b, *, pool_w, gmlp_w):
    b, l, d = x.shape
    tm = min(MATMUL_TILE, l)
    xbc_w = wx.shape[1]
    hb = tm // SUBLANES
    nhb = l // SUBLANES
    kern = functools.partial(_inproj_kernel, pool_w=pool_w, gmlp_w=gmlp_w)
    tok = lambda width: pl.BlockSpec((1, tm, width), lambda i, j: (i, j, 0))
    full = lambda a: _resident(a.shape)
    return pl.pallas_call(
        kern,
        grid=(b, l // tm),
        in_specs=[
            tok(d),
            pl.BlockSpec((1, SUBLANES, d), lambda i, j: (i, jnp.maximum(j * hb - 1, 0), 0)),
            pl.BlockSpec((1, SUBLANES, d), lambda i, j: (i, jnp.minimum(j * hb + hb, nhb - 1), 0)),
            pl.BlockSpec((1, SUBLANES, d), lambda i, j: (i, 0, 0)),
            full(g_norm), full(wx), full(wpg), full(wdt), full(conv_w8), full(conv_b),
            full(ln_g), full(ln_b), full(ws), full(bs_b),
        ],
        out_specs=[tok(xbc_w), tok(LANES), tok(pool_w), tok(gmlp_w)],
        out_shape=[
            jax.ShapeDtypeStruct((b, l, xbc_w), BF16),
            jax.ShapeDtypeStruct((b, l, LANES), F32),
            jax.ShapeDtypeStruct((b, l, pool_w), F32),
            jax.ShapeDtypeStruct((b, l, gmlp_w), BF16),
        ],
        scratch_shapes=[pltpu.VMEM((xbc_w // LANES, tm + 2 * SUBLANES, LANES), F32)],
        compiler_params=_params("parallel", "parallel"),
        name="inproj",
    )(x, x, x, modv, g_norm, wx, wpg, wdt, conv_w8, conv_b, ln_g, ln_b, ws, bs_b)


def _ssd_kernel(xab_ref, dt_ref, dtb_ref, a_ref, exp_ref, h0_ref, add_ref, y_ref, hout_ref, st,
                *, reverse, n_steps, chunks, inner):
    step = pl.program_id(1)
    bcw = SSD_GROUPS * SSD_STATE
    heads_per_group = inner // HEAD_DIM // SSD_GROUPS
    gp = heads_per_group * HEAD_DIM
    lane0 = (inner // HEAD_DIM) if reverse else 0

    @pl.when(step == 0)
    def _():
        st[...] = h0_ref[0]

    ii = lax.broadcasted_iota(jnp.int32, (CHUNK, CHUNK), 0)
    jj = lax.broadcasted_iota(jnp.int32, (CHUNK, CHUNK), 1)
    causal = (ii <= jj) if reverse else (ii >= jj)
    tri = causal.astype(BF16)
    last_row = 0 if reverse else CHUNK - 1
    head_of_lane = lax.broadcasted_iota(jnp.int32, (CHUNK, gp), 1) // HEAD_DIM

    def split(v, terms):
        out = []
        for _ in range(terms - 1):
            out.append(v.astype(BF16))
            v = v - out[-1].astype(F32)
        return out + [v.astype(BF16)]

    order = range(chunks - 1, -1, -1) if reverse else range(chunks)
    dtms, parts = {}, []
    for ci in order:
        z = dt_ref[0, ci * CHUNK:(ci + 1) * CHUNK, :] + dtb_ref[...]
        dtms[ci] = jnp.maximum(z, 0.0) + jnp.log(1.0 + jnp.exp(-jnp.abs(z)))
        parts += split(dtms[ci] * a_ref[...], 3)
    cums = jnp.dot(tri, jnp.concatenate(parts, axis=1), preferred_element_type=F32)
    acums, lhs = {}, []
    for n, ci in enumerate(order):
        c3 = cums[:, 3 * n * LANES:3 * (n + 1) * LANES]
        acum = c3[:, :LANES] + c3[:, LANES:2 * LANES] + c3[:, 2 * LANES:]
        acums[ci] = acum
        alast = acum[last_row:last_row + 1, :]
        lhs.append(jnp.concatenate(split(jnp.exp(acum), 2), axis=1))
        lhs.append(jnp.concatenate(split(dtms[ci] * jnp.exp(alast - acum), 2), axis=1))
    per_channel = jnp.dot(jnp.concatenate(lhs, axis=0), exp_ref[...], preferred_element_type=F32)
    coeffs = {}
    for n, ci in enumerate(order):
        r0 = 2 * n * CHUNK
        e2 = acums[ci] * LOG2_E
        coeffs[ci] = (e2, (e2 - jnp.log2(dtms[ci])).T, per_channel[r0:r0 + CHUNK],
                      per_channel[r0 + CHUNK:r0 + 2 * CHUNK])

    for ci in order:
        rows = slice(ci * CHUNK, (ci + 1) * CHUNK)
        e2, src_t, eac_c, wst_c = coeffs[ci]

        def cols_f32(lo, hi):
            return xab_ref[0, rows, lo:hi].astype(F32)

        def cols_bf16(lo, hi):
            return xab_ref[0, rows, lo:hi]

        for g in range(SSD_GROUPS):
            b0 = inner + g * SSD_STATE
            c0 = inner + bcw + g * SSD_STATE
            gcols = slice(g * gp, (g + 1) * gp)
            cg = cols_bf16(c0, c0 + SSD_STATE)
            bgt = cols_f32(b0, b0 + SSD_STATE).T.astype(BF16)
            cbm = jnp.dot(cg, bgt, preferred_element_type=F32)
            sg = st[g]
            yoff = jnp.dot(cg, sg.astype(BF16), preferred_element_type=F32)
            xs_f = cols_f32(g * gp, (g + 1) * gp)
            xs_b = cols_bf16(g * gp, (g + 1) * gp)
            ms, xbd = [], []
            for k in range(heads_per_group):
                ln = lane0 + g * heads_per_group + k
                seg = e2[:, ln:ln + 1] - src_t[ln:ln + 1, :]
                ms.append((cbm * jnp.exp2(jnp.where(causal, seg, -jnp.inf))).astype(BF16))
                xbd.append(jnp.where(head_of_lane == k, xs_b, jnp.zeros_like(xs_b)))
            yd = jnp.dot(jnp.concatenate(ms, axis=1), jnp.concatenate(xbd, axis=0),
                         preferred_element_type=F32)
            yg = yd + yoff * eac_c[:, gcols]
            if reverse:
                yg = yg + add_ref[0, rows, gcols]
            else:
                yg = yg + add_ref[:, gcols] * xs_f
            y_ref[0, rows, gcols] = yg
            xd = (xs_f * wst_c[:, gcols]).astype(BF16)
            st[g] = (sg * eac_c[last_row:last_row + 1, gcols]
                     + jnp.dot(bgt, xd, preferred_element_type=F32))

    @pl.when(step == n_steps - 1)
    def _():
        hout_ref[0] = st[...]


def _head_expand(inner, lane0):
    e = np.zeros((LANES, inner), np.float32)
    for h in range(inner // HEAD_DIM):
        e[lane0 + h, h * HEAD_DIM:(h + 1) * HEAD_DIM] = 1.0
    return jnp.asarray(np.concatenate([e, e], axis=0), BF16)


def _ssd_scan(xab, dtr, dtb_row, a_row, h0, add, *, reverse, inner):
    b, l, xw_cols = xab.shape
    chunks = min(SSD_CHUNKS_PER_STEP, l // CHUNK)
    n_steps = l // (chunks * CHUNK)
    gp = inner // SSD_GROUPS
    block_of = (lambda j: n_steps - 1 - j) if reverse else (lambda j: j)
    cur = lambda width: pl.BlockSpec((1, chunks * CHUNK, width), lambda i, j: (i, block_of(j), 0))
    state = pl.BlockSpec((1, SSD_GROUPS, SSD_STATE, gp), lambda i, j: (i, 0, 0, 0))
    lane0 = (inner // HEAD_DIM) if reverse else 0
    kern = functools.partial(_ssd_kernel, reverse=reverse, n_steps=n_steps, chunks=chunks,
                             inner=inner)
    return pl.pallas_call(
        kern,
        grid=(b, n_steps),
        in_specs=[cur(xw_cols), cur(LANES), _resident((1, LANES)), _resident((1, LANES)),
                  _resident((2 * LANES, inner)), state,
                  cur(inner) if reverse else _resident((1, inner))],
        out_specs=[cur(inner), state],
        out_shape=[jax.ShapeDtypeStruct((b, l, inner), F32),
                   jax.ShapeDtypeStruct((b, SSD_GROUPS, SSD_STATE, gp), F32)],
        scratch_shapes=[pltpu.VMEM((SSD_GROUPS, SSD_STATE, gp), F32)],
        compiler_params=_params("parallel", "arbitrary"),
        name="ssd_bwd" if reverse else "ssd_fwd",
    )(xab, dtr, dtb_row, a_row, _head_expand(inner, lane0), h0, add)


def _pool_kernel(pp_ref, band_ref, cinv_ref, inv_ref, pw_ref, ps_ref, o_ref, cp, *, width, rows,
                 slab):
    g = pl.program_id(1)
    pad = SUBLANES * width
    l = rows * width
    rows_per_slab = slab // width

    def body(window):
        lo = window // 2
        hi = window - lo
        cp[0:pad, :] = jnp.zeros((pad, LANES), F32)
        cp[pad + l:, :] = jnp.zeros((pad, LANES), F32)

        def col_pool(s, carry):
            off = pl.multiple_of(s * slab, slab)
            v = pp_ref[0, pl.ds(off, slab), :]
            v_hi = v.astype(BF16)
            v_lo = (v - v_hi.astype(F32)).astype(BF16)
            sums = jnp.dot(band_ref[0], jnp.concatenate([v_hi, v_lo], axis=1),
                           preferred_element_type=F32)
            cp[pl.ds(pad + off, slab), :] = (sums[:, :LANES] + sums[:, LANES:]) * cinv_ref[0]
            return carry

        lax.fori_loop(0, l // slab, col_pool, 0, unroll=min(4, l // slab))

        def row_pool(s, carry):
            off = pl.multiple_of(s * slab, slab)
            parts = []
            for r in range(rows_per_slab):
                base = pad + off + r * width
                acc = cp[pl.ds(base - lo * width, width), :]
                for k in range(1 - lo, hi):
                    acc = acc + cp[pl.ds(base + k * width, width), :]
                parts.append(acc * inv_ref[0, pl.ds(s * rows_per_slab + r, 1), :])
            pooled = parts[0] if len(parts) == 1 else jnp.concatenate(parts, axis=0)
            d = (pooled - pp_ref[0, pl.ds(off, slab), :]).astype(BF16)
            y = jnp.dot(d, pw_ref[0], preferred_element_type=F32) * ps_ref[0]
            o_ref[0, pl.ds(off, slab), :] = y.astype(BF16)
            return carry

        lax.fori_loop(0, l // slab, row_pool, 0, unroll=min(4, l // slab))

    for gi, window in enumerate(POOL_WINDOWS):
        pl.when(g == gi)(functools.partial(body, window))


def _pool_tables(l, width):
    rows = l // width
    slab = max(2 * CHUNK, width)
    n_g = len(POOL_WINDOWS)
    band = np.zeros((n_g, slab, slab), np.float32)
    cinv = np.ones((n_g, slab, LANES), np.float32)
    inv = np.ones((n_g, max(rows, SUBLANES), LANES), np.float32)
    for gi, w in enumerate(POOL_WINDOWS):
        lo, hi = w // 2, w - w // 2
        for t in range(slab):
            r, c = divmod(t, width)
            c0, c1 = max(c - lo, 0), min(c + hi, width)
            band[gi, t, r * width + c0:r * width + c1] = 1.0
            cinv[gi, t, :] = 1.0 / (c1 - c0)
        for r in range(rows):
            inv[gi, r, :] = 1.0 / (min(r + hi, rows) - max(r - lo, 0))
    return jnp.asarray(band, BF16), jnp.asarray(cinv), jnp.asarray(inv), rows, slab


def _pool_branch(pp, pool_w, pool_scale, *, width):
    b, l, pw = pp.shape
    n_g = len(POOL_WINDOWS)
    band, cinv, inv, rows, slab = _pool_tables(l, width)
    kern = functools.partial(_pool_kernel, width=width, rows=rows, slab=slab)
    return pl.pallas_call(
        kern,
        grid=(b, n_g),
        in_specs=[
            pl.BlockSpec((1, l, LANES), lambda i, g: (i, 0, g)),
            pl.BlockSpec((1, slab, slab), lambda i, g: (g, 0, 0)),
            pl.BlockSpec((1, slab, LANES), lambda i, g: (g, 0, 0)),
            pl.BlockSpec((1,) + inv.shape[1:], lambda i, g: (g, 0, 0)),
            pl.BlockSpec((1, LANES, LANES), lambda i, g: (g, 0, 0)),
            pl.BlockSpec((1, 1, LANES), lambda i, g: (g, 0, 0)),
        ],
        out_specs=pl.BlockSpec((1, l, LANES), lambda i, g: (i, 0, g)),
        out_shape=jax.ShapeDtypeStruct((b, l, pw), BF16),
        scratch_shapes=[pltpu.VMEM((l + 2 * SUBLANES * width, LANES), F32)],
        compiler_params=_params("parallel", "parallel"),
        name="pool",
    )(pp, band, cinv, inv, pool_w, pool_scale.reshape(n_g, 1, LANES))


def _mix_kernel(x_ref, mod_ref, g1_ref, g2_ref, wz_ref, wg_ref, y_ref, sg_ref, yp_ref, ygm_ref,
                wbs_ref, wbp_ref, wbg_ref, wo_ref, wr_ref, br_ref, x1_ref, h2_ref):
    x = x_ref[0]
    d = x.shape[1]
    h = _modnorm(x, g1_ref[...], mod_ref[0, 0:1, :], mod_ref[0, 1:2, :]).astype(BF16)
    z = jnp.dot(h, wz_ref[...], preferred_element_type=F32)
    t = y_ref[0] * _silu(z)
    ys = t * lax.rsqrt(jnp.mean(t * t, axis=-1, keepdims=True) + SSD_NORM_EPS) * sg_ref[...]
    gates = _sigmoid(jnp.dot(h, wg_ref[...], preferred_element_type=F32))
    merged = gates[:, :d] * jnp.dot(ys.astype(BF16), wbs_ref[...], preferred_element_type=F32)
    merged = merged + gates[:, d:2 * d] * jnp.dot(yp_ref[0], wbp_ref[...], preferred_element_type=F32)
    merged = merged + gates[:, 2 * d:] * jnp.dot(ygm_ref[0], wbg_ref[...], preferred_element_type=F32)
    mix = jnp.dot(merged.astype(BF16), wo_ref[...], preferred_element_type=F32)
    x1 = x + mod_ref[0, 2:3, :] * mix
    x1_ref[0] = x1
    h2 = _modnorm(x1, g2_ref[...], mod_ref[0, 3:4, :], mod_ref[0, 4:5, :])
    h2_ref[0, :, :d] = h2

    h2_hi = h2.astype(BF16)
    h2_lo = (h2 - h2_hi.astype(F32)).astype(BF16)
    p_hi = jnp.dot(h2_hi, wr_ref[...], preferred_element_type=F32)
    p_lo = jnp.dot(h2_lo, wr_ref[:, :LANES], preferred_element_type=F32)
    lg = p_hi[:, :LANES] + p_hi[:, LANES:] + p_lo + br_ref[...]
    lane = lax.broadcasted_iota(jnp.int32, lg.shape, 1).astype(F32)
    ninf = -jnp.inf
    big = float(LANES)
    gl = jnp.where(lane < MOE_GROUPS, lg, ninf)
    gmax = jnp.max(gl, axis=-1, keepdims=True)
    pg_top = 1.0 / jnp.sum(jnp.exp(gl - gmax), axis=-1, keepdims=True)
    gidx = jnp.min(jnp.where(gl == gmax, lane, big), axis=-1, keepdims=True)
    e0 = MOE_GROUPS + EXPERTS_PER_GROUP * gidx
    el = jnp.where((lane >= e0) & (lane < e0 + EXPERTS_PER_GROUP), lg, ninf)
    v1 = jnp.max(el, axis=-1, keepdims=True)
    i1 = jnp.min(jnp.where(el == v1, lane, big), axis=-1, keepdims=True)
    el2 = jnp.where(lane == i1, ninf, el)
    v2 = jnp.max(el2, axis=-1, keepdims=True)
    i2 = jnp.min(jnp.where(el2 == v2, lane, big), axis=-1, keepdims=True)
    r = jnp.exp(v2 - v1)
    wt1 = pg_top * (1.0 / (1.0 + r))
    wt2 = pg_top * (r / (1.0 + r))
    first_lo = i1 < i2
    lo = jnp.where(first_lo, i1, i2) - e0
    hi = jnp.where(first_lo, i2, i1) - e0
    w_lo = jnp.where(first_lo, wt1, wt2)
    w_hi = jnp.where(first_lo, wt2, wt1)
    cls = gidx * N_PAIRS + lo * (7.0 - lo) * 0.5 + (hi - lo - 1.0)
    info = jnp.where(lane == cls, 1.0, 0.0)
    info = jnp.where(lane == LANE_W_LO, w_lo, info)
    info = jnp.where(lane == LANE_W_HI, w_hi, info)
    h2_ref[0, :, d:] = info


def _mix(x, modv, g1, g2, wz, wg, y, ssd_g, yp, ygm, wbs, wbp, wbg, wo, wr, br):
    b, l, d = x.shape
    tm = min(MATMUL_TILE, l)
    tok = lambda width: pl.BlockSpec((1, tm, width), lambda i, j: (i, j, 0))
    full = lambda a: _resident(a.shape)
    return pl.pallas_call(
        _mix_kernel,
        grid=(b, l // tm),
        in_specs=[
            tok(d), pl.BlockSpec((1, SUBLANES, d), lambda i, j: (i, 0, 0)),
            full(g1), full(g2), full(wz), full(wg), tok(y.shape[2]), full(ssd_g),
            tok(yp.shape[2]), tok(ygm.shape[2]), full(wbs), full(wbp), full(wbg), full(wo),
            full(wr), full(br),
        ],
        out_specs=[tok(d), tok(d + LANES)],
        out_shape=[jax.ShapeDtypeStruct((b, l, d), F32),
                   jax.ShapeDtypeStruct((b, l, d + LANES), F32)],
        compiler_params=_params("parallel", "parallel"),
        name="mix",
    )(x, modv, g1, g2, wz, wg, y, ssd_g, yp, ygm, wbs, wbp, wbg, wo, wr, br)


def _sort_kernel(info_ref, pos_ref, tcls_ref, cnt, base, *, tile, n_slabs, n_tile_lanes):
    phase = pl.program_id(0)
    i = pl.program_id(1)
    ts = info_ref.shape[0]
    sub = lax.broadcasted_iota(jnp.int32, (LANES, LANES), 0)
    lan = lax.broadcasted_iota(jnp.int32, (LANES, LANES), 1)

    @pl.when((phase == 0) & (i == 0))
    def _():
        cnt[...] = jnp.zeros_like(cnt)

    @pl.when(phase == 0)
    def _():
        lane = lax.broadcasted_iota(jnp.int32, (ts, LANES), 1)
        one_hot = jnp.where(lane < N_CLASSES, info_ref[...], 0.0)
        cnt[...] += jnp.broadcast_to(jnp.sum(one_hot, axis=0, keepdims=True), cnt.shape)

    @pl.when((phase == 1) & (i == 0))
    def _():
        counts = cnt[...].T
        padded = jnp.ceil(counts * (1.0 / tile)) * tile
        starts = jnp.dot((lan < sub).astype(F32), padded, precision=HIGHEST,
                         preferred_element_type=F32)
        base[...] = starts
        tcls_ref[...] = jnp.zeros_like(tcls_ref)
        ends = (starts + padded)[:, 0:1]
        tl = lax.broadcasted_iota(jnp.int32, (LANES, n_tile_lanes), 1).astype(F32) * tile
        cs = lax.broadcasted_iota(jnp.int32, (LANES, n_tile_lanes), 0)
        done = jnp.where((cs < N_CLASSES) & (ends <= tl), 1.0, 0.0)
        tc = jnp.minimum(jnp.sum(done, axis=0, keepdims=True), N_CLASSES - 1.0)
        total = jnp.max(jnp.where(cs < N_CLASSES, ends, 0.0), axis=0, keepdims=True)
        valid = jnp.where(tl[0:1, :] < total, 1.0, 0.0)
        tcls_ref[0, 0:1, :] = tc.astype(jnp.int32)
        tcls_ref[0, 1:2, :] = valid.astype(jnp.int32)
        last_tile = jnp.where((sub < N_CLASSES) & (padded > 0.0), starts + padded - tile, -1.0)
        in_use = jnp.max(jnp.where(sub < N_CLASSES, starts + padded, 0.0), axis=0, keepdims=True)
        last_tile = jnp.where(sub == N_CLASSES, in_use, last_tile)
        tcls_ref[0, 2:3, 0:LANES] = last_tile.T[0:1, :].astype(jnp.int32)

    @pl.when(phase == 1)
    def _():
        oh_t = jnp.concatenate(
            [info_ref[k * LANES:(k + 1) * LANES, :].T for k in range(n_slabs)], axis=1)
        csub = lax.broadcasted_iota(jnp.int32, oh_t.shape, 0)
        oh_t = jnp.where(csub < N_CLASSES, oh_t, 0.0)
        ti = lax.broadcasted_iota(jnp.int32, (ts, ts), 0)
        tj = lax.broadcasted_iota(jnp.int32, (ts, ts), 1)
        before = jnp.dot(oh_t.astype(BF16), (ti < tj).astype(BF16), preferred_element_type=F32)
        posv = jnp.sum(oh_t * (before + base[:, 0:1]), axis=0, keepdims=True)
        pos_ref[0] = posv.astype(jnp.int32)
        base[...] += jnp.broadcast_to(jnp.sum(oh_t, axis=-1, keepdims=True), base.shape)


def _route_sort(h2ext, *, d, n_exp_tiles, tile):
    t = h2ext.shape[0]
    ts = min(1024, t)
    nt = t // ts
    n_tile_lanes = -(-n_exp_tiles // LANES) * LANES
    kern = functools.partial(_sort_kernel, tile=tile, n_slabs=ts // LANES,
                             n_tile_lanes=n_tile_lanes)
    pos, tcls = pl.pallas_call(
        kern,
        grid=(2, nt),
        in_specs=[pl.BlockSpec((ts, LANES), lambda p, i: (i, d // LANES))],
        out_specs=[pl.BlockSpec((1, 1, ts), lambda p, i: (i * p, 0, 0)),
                   pl.BlockSpec((1, SUBLANES, n_tile_lanes), lambda p, i: (0, 0, 0))],
        out_shape=[jax.ShapeDtypeStruct((nt, 1, ts), jnp.int32),
                   jax.ShapeDtypeStruct((1, SUBLANES, n_tile_lanes), jnp.int32)],
        scratch_shapes=[pltpu.VMEM((LANES, LANES), F32), pltpu.VMEM((LANES, LANES), F32)],
        compiler_params=_params("arbitrary", "arbitrary"),
        name="route_sort",
    )(h2ext)
    return (pos.reshape(t), tcls[0, 0, :n_exp_tiles], tcls[0, 1, :n_exp_tiles],
            tcls[0, 2, :N_CLASSES + 1])


def _row_copies(n_groups, make_copy):
    def issue(g, carry):
        for j in range(SUBLANES):
            make_copy(g, j).start(priority=j % 2)
        return carry

    lax.fori_loop(0, n_groups, issue, 0)


def _row_of(ref, r):
    return ref.at[lax.shift_right_logical(r, 3), pl.ds(lax.bitwise_and(r, SUBLANES - 1), 1)]


def _dispatch_kernel(pad_ref, pos_ref, rows_ref, out_ref, zeros, sem, zsem):
    n_groups = rows_ref.shape[0]

    @pl.when(pl.program_id(0) == 0)
    def _():
        zeros[...] = jnp.zeros_like(zeros)
        n_rows = out_ref.shape[0] * SUBLANES
        tile_groups = zeros.shape[0]
        tile = tile_groups * SUBLANES
        starts = [pad_ref[c] for c in range(N_CLASSES)]
        starts += [pad_ref[N_CLASSES] + k * tile for k in range(N_CLASSES)]
        wanted = [s >= 0 for s in starts[:N_CLASSES]] + [s < n_rows for s in starts[N_CLASSES:]]

        def fill(s):
            g0 = pl.multiple_of(
                lax.shift_right_logical(jnp.clip(s, 0, n_rows - tile), 3), tile_groups)
            return pltpu.make_async_copy(zeros, out_ref.at[pl.ds(g0, tile_groups)], zsem)

        for s, w in zip(starts, wanted):
            pl.when(w)(lambda s=s: fill(s).start())
        for s, w in zip(starts, wanted):
            pl.when(w)(lambda s=s: fill(s).wait())

    _row_copies(n_groups, lambda g, j: pltpu.make_async_copy(
        rows_ref.at[g, pl.ds(j, 1)], _row_of(out_ref, pos_ref[0, 0, g * SUBLANES + j]), sem))
    pltpu.make_async_copy(rows_ref, out_ref.at[pl.ds(0, n_groups)], sem).wait()


def _dispatch(h2ext, pos, pad_rows, *, n_sorted_rows, tile):
    t, w = h2ext.shape
    tm = min(ROW_TILE, t)
    nt = t // tm
    grid_spec = pltpu.PrefetchScalarGridSpec(
        num_scalar_prefetch=1,
        grid=(nt,),
        in_specs=[
            pl.BlockSpec((1, 1, tm), lambda i, pad: (i, 0, 0), memory_space=pltpu.SMEM),
            pl.BlockSpec((tm // SUBLANES, SUBLANES, w), lambda i, pad: (i, 0, 0)),
        ],
        out_specs=pl.BlockSpec(memory_space=pl.ANY),
        scratch_shapes=[pltpu.VMEM((tile // SUBLANES, SUBLANES, w), F32),
                        pltpu.SemaphoreType.DMA, pltpu.SemaphoreType.DMA],
    )
    out = pl.pallas_call(
        _dispatch_kernel,
        grid_spec=grid_spec,
        out_shape=jax.ShapeDtypeStruct((n_sorted_rows // SUBLANES, SUBLANES, w), F32),
        compiler_params=_params("arbitrary"),
        name="dispatch",
    )(pad_rows, pos.reshape(nt, 1, tm), h2ext.reshape(t // SUBLANES, SUBLANES, w))
    return out.reshape(n_sorted_rows, w)


def _expert_kernel(elo_ref, ehi_ref, valid_ref, rows_ref, wil_ref, wih_ref, wol_ref, woh_ref,
                   o_ref, *, d, d_expert):
    del elo_ref, ehi_ref
    i = pl.program_id(0)

    @pl.when(valid_ref[i] == 0)
    def _():
        o_ref[...] = jnp.zeros_like(o_ref)

    @pl.when(valid_ref[i] != 0)
    def _():
        h = rows_ref[:, :d].astype(BF16)
        info = rows_ref[:, d:]

        def expert(wi_ref, wo_ref):
            gu = jnp.dot(h, wi_ref[0], preferred_element_type=F32)
            hid = _silu(gu[:, :d_expert]) * gu[:, d_expert:]
            return jnp.dot(hid.astype(BF16), wo_ref[0], preferred_element_type=F32)

        o_ref[...] = (info[:, LANE_W_LO:LANE_W_LO + 1] * expert(wil_ref, wol_ref)
                      + info[:, LANE_W_HI:LANE_W_HI + 1] * expert(wih_ref, woh_ref))


def _experts(sorted_rows, e_lo, e_hi, valid, w_e_in, w_e_out, *, d, tile):
    n_rows, w = sorted_rows.shape
    tm = tile
    n_tiles = n_rows // tm
    d_expert = w_e_out.shape[1]
    kern = functools.partial(_expert_kernel, d=d, d_expert=d_expert)
    grid_spec = pltpu.PrefetchScalarGridSpec(
        num_scalar_prefetch=3,
        grid=(n_tiles,),
        in_specs=[
            pl.BlockSpec((tm, w), lambda i, lo, hi, v: (i * v[i], 0)),
            pl.BlockSpec((1,) + w_e_in.shape[1:], lambda i, lo, hi, v: (lo[i], 0, 0)),
            pl.BlockSpec((1,) + w_e_in.shape[1:], lambda i, lo, hi, v: (hi[i], 0, 0)),
            pl.BlockSpec((1,) + w_e_out.shape[1:], lambda i, lo, hi, v: (lo[i], 0, 0)),
            pl.BlockSpec((1,) + w_e_out.shape[1:], lambda i, lo, hi, v: (hi[i], 0, 0)),
        ],
        out_specs=pl.BlockSpec((tm, d), lambda i, lo, hi, v: (i, 0)),
    )
    return pl.pallas_call(
        kern,
        grid_spec=grid_spec,
        out_shape=jax.ShapeDtypeStruct((n_rows, d), F32),
        compiler_params=_params("arbitrary"),
        name="experts",
    )(e_lo, e_hi, valid, sorted_rows, w_e_in, w_e_in, w_e_out, w_e_out)


def _combine_kernel(pos_ref, x1_ref, mod_ref, gf_ref, moe_ref, o_ref, rows, sem, *, final):
    n_groups = rows.shape[0]
    _row_copies(n_groups, lambda g, j: pltpu.make_async_copy(
        _row_of(moe_ref, pos_ref[0, 0, g * SUBLANES + j]), rows.at[g, pl.ds(j, 1)], sem))
    pltpu.make_async_copy(moe_ref.at[pl.ds(0, n_groups)], rows, sem).wait()
    x2 = x1_ref[...] + mod_ref[0, 5:6, :] * rows[...]
    if final:
        x2 = x2 * lax.rsqrt(jnp.mean(x2 * x2, axis=-1, keepdims=True) + RMS_EPS) * gf_ref[...]
    o_ref[...] = x2


def _combine(x1, pos, modv, g_final, moe_sorted, *, seq_len, final):
    t, d = x1.shape
    tm = min(ROW_TILE, seq_len)
    nt = t // tm
    tiles_per_seq = seq_len // tm
    kern = functools.partial(_combine_kernel, final=final)
    grouped = lambda a: a.reshape(a.shape[0] // SUBLANES, SUBLANES, a.shape[1])
    tile = pl.BlockSpec((tm // SUBLANES, SUBLANES, d), lambda i: (i, 0, 0))
    out = pl.pallas_call(
        kern,
        grid=(nt,),
        in_specs=[
            pl.BlockSpec((1, 1, tm), lambda i: (i, 0, 0), memory_space=pltpu.SMEM),
            tile,
            pl.BlockSpec((1, SUBLANES, d), lambda i: (i // tiles_per_seq, 0, 0)),
            pl.BlockSpec((1, d), lambda i: (0, 0)),
            pl.BlockSpec(memory_space=pl.ANY),
        ],
        out_specs=tile,
        out_shape=jax.ShapeDtypeStruct((t // SUBLANES, SUBLANES, d), F32),
        scratch_shapes=[pltpu.VMEM((tm // SUBLANES, SUBLANES, d), F32), pltpu.SemaphoreType.DMA],
        compiler_params=_params("arbitrary"),
        name="combine",
    )(pos.reshape(nt, 1, tm), grouped(x1), modv, g_final, grouped(moe_sorted))
    return out.reshape(t, d)


def _pair_tables():
    lo, hi = [], []
    for g in range(MOE_GROUPS):
        for a in range(EXPERTS_PER_GROUP):
            for b in range(a + 1, EXPERTS_PER_GROUP):
                lo.append(g * EXPERTS_PER_GROUP + a)
                hi.append(g * EXPERTS_PER_GROUP + b)
    return jnp.asarray(lo, jnp.int32), jnp.asarray(hi, jnp.int32)


def _stream_layer(x, modv, lw, h0f, h0b, *, width, final, g_final, need_tokens=True):
    b, l, d = x.shape
    inner = lw["inner"]
    xab, dtr, pp, ygm = _inproj(x, modv, lw["g1"], lw["wx"], lw["wpg"], lw["wdt"], lw["conv_w8"],
                                lw["conv_b"], lw["ln_g"], lw["ln_b"], lw["ws"], lw["bs_b"],
                                pool_w=lw["pool_cols"], gmlp_w=lw["gmlp_w"])
    yf, hf = _ssd_scan(xab, dtr, lw["dtb_row"], lw["a_row"], h0f, lw["dsk"], reverse=False,
                       inner=inner)
    y, hb = _ssd_scan(xab, dtr, lw["dtb_row"], lw["a_row"], h0b, yf, reverse=True, inner=inner)
    if not need_tokens:
        return None, hf, hb
    yp = _pool_branch(pp, lw["pool_w"], lw["pool_scale"], width=width)
    x1, h2ext = _mix(x, modv, lw["g1"], lw["g2"], lw["wz"], lw["wg"], y, lw["ssd_g"], yp, ygm,
                     lw["wbs"], lw["wbp"], lw["wbg"], lw["wo"], lw["wr"], lw["br"])
    t = b * l
    h2ext = h2ext.reshape(t, d + LANES)
    tile = EXPERT_TILE if t >= SMALL_STREAM_TOKENS else SMALL_EXPERT_TILE
    n_exp_tiles = t // tile + N_CLASSES
    pos, tcls, valid, pad_rows = _route_sort(h2ext, d=d, n_exp_tiles=n_exp_tiles, tile=tile)
    pair_lo, pair_hi = _pair_tables()
    sorted_rows = _dispatch(h2ext, pos, pad_rows, n_sorted_rows=n_exp_tiles * tile, tile=tile)
    moe_sorted = _experts(sorted_rows, pair_lo[tcls], pair_hi[tcls], valid, lw["w_e_in"],
                          lw["w_e_out"], d=d, tile=tile)
    x2 = _combine(x1.reshape(t, d), pos, modv, g_final, moe_sorted, seq_len=l, final=final)
    return x2.reshape(b, l, d), hf, hb


def kernel(x, c, ctx, c_ctx, w_mod, b_mod, g_norm1, g_norm2, w_in, conv_w, conv_b, dt_bias, a_log,
           d_skip, ssd_norm_g, pool_w, pool_scale, gmlp_ln_g, gmlp_ln_b, gmlp_ws, gmlp_bs, w_br_ssd,
           w_br_pool, w_br_gmlp, w_o, w_rg, b_rg, w_re, b_re, w_e_in, w_e_out, g_final):
    b, seq, d = x.shape
    depth = w_mod.shape[0]
    n_heads = d_skip.shape[1]
    inner = n_heads * HEAD_DIM
    bcw = SSD_GROUPS * SSD_STATE
    xbc_w = inner + 2 * bcw
    pool_cols = pool_w.shape[1] * pool_w.shape[2]
    gmlp_w = gmlp_ln_g.shape[1]
    off_xbc = inner
    off_dt = off_xbc + xbc_w
    off_pool = off_dt + 2 * n_heads
    off_gmlp = off_pool + pool_cols
    off_gate = off_gmlp + 2 * gmlp_w

    rows = jnp.zeros((SUBLANES, d), F32).at[:b].set(c).at[b].set(c_ctx)
    mod = _modulation(rows, w_mod, b_mod)
    gp = inner // SSD_GROUPS
    zero_state = jnp.zeros((b, SSD_GROUPS, SSD_STATE, gp), F32)
    g_final_row = g_final.reshape(1, d)

    def pad_rows(m):
        return jnp.pad(m.reshape(m.shape[0], 6, d), ((0, 0), (0, 2), (0, 0)))

    for i in range(depth):
        last = i == depth - 1
        mod_lat = pad_rows(mod[i, :b])
        mod_ctx = pad_rows(jnp.broadcast_to(mod[i, b:b + 1], (b, 6 * d)))
        wi = w_in[i]
        a = -jnp.exp(a_log[i].astype(F32)).reshape(1, 2 * n_heads)
        lane_pad = ((0, 0), (0, LANES - 2 * n_heads))
        wr = jnp.pad(jnp.concatenate([w_rg[i], w_re[i]], axis=1),
                     ((0, 0), (0, LANES - MOE_GROUPS - MOE_GROUPS * EXPERTS_PER_GROUP)))
        br = jnp.pad(jnp.concatenate([b_rg[i], b_re[i]]).reshape(1, -1),
                     ((0, 0), (0, LANES - MOE_GROUPS - MOE_GROUPS * EXPERTS_PER_GROUP)))
        lw = dict(
            inner=inner, xbc_w=xbc_w, pool_cols=pool_cols, gmlp_w=gmlp_w,
            g1=g_norm1[i].reshape(1, d), g2=g_norm2[i].reshape(1, d),
            wx=wi[:, off_xbc:off_dt].astype(BF16), wpg=wi[:, off_pool:off_gate].astype(BF16),
            wdt=jnp.pad(wi[:, off_dt:off_pool], lane_pad).astype(BF16),
            wz=wi[:, :inner].astype(BF16), wg=wi[:, off_gate:].astype(BF16),
            ln_g=gmlp_ln_g[i].reshape(1, gmlp_w), ln_b=gmlp_ln_b[i].reshape(1, gmlp_w),
            ws=gmlp_ws[i].astype(BF16),
            bs_b=jnp.broadcast_to(gmlp_bs[i][:, :, None], gmlp_bs[i].shape + (LANES,)),
            conv_w8=jnp.pad(conv_w[i], ((0, SUBLANES - D_CONV), (0, 0))),
            conv_b=conv_b[i].reshape(1, xbc_w),
            dtb_row=jnp.pad(dt_bias[i].reshape(1, 2 * n_heads), lane_pad),
            a_row=jnp.pad(a, lane_pad),
            dsk=jnp.repeat(d_skip[i], HEAD_DIM).reshape(1, inner),
            ssd_g=ssd_norm_g[i].reshape(1, inner),
            pool_w=pool_w[i].astype(BF16), pool_scale=pool_scale[i],
            wbs=w_br_ssd[i].astype(BF16), wbp=w_br_pool[i].astype(BF16),
            wbg=w_br_gmlp[i].astype(BF16), wo=w_o[i].astype(BF16),
            wr=jnp.concatenate([wr.astype(BF16),
                                (wr - wr.astype(BF16).astype(F32)).astype(BF16)], axis=1),
            br=br,
            w_e_in=w_e_in[i].astype(BF16), w_e_out=w_e_out[i].astype(BF16),
        )
        ctx_new, hf_c, hb_c = _stream_layer(ctx, mod_ctx, lw, zero_state, zero_state,
                                            width=ctx.shape[1], final=False, g_final=g_final_row,
                                            need_tokens=not last)
        x, _, _ = _stream_layer(x, mod_lat, lw, hf_c, hb_c, width=GRID_W, final=last,
                                g_final=g_final_row)
        if not last:
            ctx = ctx_new
    return x
```
